```python
import jax, jax.numpy as jnp
from jax import lax
import numpy as np

D_MODEL = 2048
BATCH = 2
SEQ = 4096
DEPTH = 4

CHUNK = 128
SGU_GROUPS = 8
SGU_GROUP_DIM = 128
SGU_WIDTH = SGU_GROUPS * SGU_GROUP_DIM
RET_HEADS = 8
RET_DK = 128
RET_DV = 256
RET_QK_WIDTH = RET_HEADS * RET_DK
RET_V_WIDTH = RET_HEADS * RET_DV
ROPE_THETA = 10000.0
D_FF = 5632
CONV_W = 3
PLE_DIM = 256
NORM_EPS = 1e-6
IN_SIZES = (SGU_WIDTH, SGU_WIDTH, RET_QK_WIDTH, RET_QK_WIDTH, RET_V_WIDTH, RET_V_WIDTH, D_MODEL, D_MODEL)
IN_COLS = sum(IN_SIZES)
IN_OFFSETS = tuple(int(o) for o in np.cumsum(IN_SIZES)[:-1])

kernel_name = "hybrid_sgu_retention_convffn_ple"


def rms_norm(x, g):
    xf = x.astype(jnp.float32)
    y = xf * lax.rsqrt(jnp.mean(xf * xf, axis=-1, keepdims=True) + NORM_EPS)
    return (y * g.astype(jnp.float32)).astype(x.dtype)


def rotary(x, pos):
    half = x.shape[-1] // 2
    inv = ROPE_THETA ** (-jnp.arange(half, dtype=jnp.float32) / half)
    ang = pos.astype(jnp.float32)[:, None] * inv[None, :]
    cos = jnp.cos(ang)[None, :, None, :].astype(x.dtype)
    sin = jnp.sin(ang)[None, :, None, :].astype(x.dtype)
    x1, x2 = x[..., :half], x[..., half:]
    return jnp.concatenate([x1 * cos - x2 * sin, x2 * cos + x1 * sin], axis=-1)


def spatial_gating(u, v, norm_g, w_s, b_s):
    B, S, _ = v.shape
    n = S // CHUNK
    v = rms_norm(v, norm_g)
    vg = v.reshape(B, n, CHUNK, SGU_GROUPS, SGU_GROUP_DIM)
    mask = jnp.tril(jnp.ones((CHUNK, CHUNK), dtype=w_s.dtype))
    w = w_s * mask[None]
    s = jnp.einsum('gts,bnsgd->bntgd', w, vg) + b_s.T[:, :, None]
    return u * s.reshape(B, S, SGU_WIDTH)


def retention(q, k, v):
    B, S, H, DK = q.shape
    DV = v.shape[-1]
    n = S // CHUNK
    q = q.reshape(B, n, CHUNK, H, DK)
    k = k.reshape(B, n, CHUNK, H, DK)
    v = v.reshape(B, n, CHUNK, H, DV)
    log_g = jnp.log1p(-jnp.exp2(-5.0 - jnp.arange(H, dtype=jnp.float32)))
    idx = jnp.arange(CHUNK, dtype=jnp.float32)
    diff = idx[:, None] - idx[None, :]
    causal = diff >= 0
    decay = jnp.where(causal[None], jnp.exp(log_g[:, None, None] * jnp.where(causal, diff, 0.0)[None]), 0.0)
    xi = jnp.exp(log_g[:, None] * (idx[None, :] + 1.0)).T
    zeta = jnp.exp(log_g[:, None] * (CHUNK - 1.0 - idx[None, :])).T
    chunk_decay = jnp.exp(log_g * CHUNK)
    scores = jnp.einsum('bnthd,bnshd->bnhts', q, k) * decay.astype(q.dtype)[None, None]
    inner = jnp.einsum('bnhts,bnshe->bnthe', scores, v)
    kv = jnp.einsum('bnshd,bnshe->bnhde', k * zeta.astype(k.dtype)[None, None, :, :, None], v)
    cd = chunk_decay.astype(kv.dtype)[None, :, None, None]

    def step(state, kv_n):
        return state * cd + kv_n, state

    _, prev = lax.scan(step, jnp.zeros((B, H, DK, DV), kv.dtype), jnp.moveaxis(kv, 1, 0))
    prev = jnp.moveaxis(prev, 0, 1)
    cross = jnp.einsum('bnthd,bnhde->bnthe', q, prev) * xi.astype(q.dtype)[None, None, :, :, None]
    return (inner + cross).reshape(B, S, H, DV)


def causal_dwconv(a, w, b):
    S = a.shape[1]
    K = w.shape[0]
    ap = jnp.pad(a, ((0, 0), (K - 1, 0), (0, 0)))
    out = ap[:, 0:S] * w[0]
    for j in range(1, K):
        out = out + ap[:, j:j + S] * w[j]
    return out + b


def setup_inputs(seed: int = 0) -> dict:
    key = jax.random.key(seed)
    ks = jax.random.split(key, 22)
    f32 = jnp.float32

    def nrm(k, shape, scale):
        return jax.random.normal(k, shape, f32) * scale

    def gain(k, shape):
        return 1.0 + 0.02 * jax.random.normal(k, shape, f32)

    return {
        "x": nrm(ks[0], (BATCH, SEQ, D_MODEL), 1.0),
        "p": nrm(ks[1], (DEPTH, BATCH, SEQ, PLE_DIM), 1.0),
        "mix_norm_g": gain(ks[2], (DEPTH, D_MODEL)),
        "w_in": nrm(ks[3], (DEPTH, D_MODEL, IN_COLS), D_MODEL ** -0.5),
        "sgu_norm_g": gain(ks[4], (DEPTH, SGU_WIDTH)),
        "sgu_w": nrm(ks[5], (DEPTH, SGU_GROUPS, CHUNK, CHUNK), CHUNK ** -0.5),
        "sgu_b": nrm(ks[6], (DEPTH, SGU_GROUPS, CHUNK), 0.02),
        "ret_norm_g": gain(ks[7], (DEPTH, RET_V_WIDTH)),
        "w_branch_a": nrm(ks[8], (DEPTH, SGU_WIDTH, D_MODEL), SGU_WIDTH ** -0.5),
        "w_branch_b": nrm(ks[9], (DEPTH, RET_V_WIDTH, D_MODEL), RET_V_WIDTH ** -0.5),
        "w_out": nrm(ks[10], (DEPTH, D_MODEL, D_MODEL), D_MODEL ** -0.5),
        "ffn_norm_g": gain(ks[11], (DEPTH, D_MODEL)),
        "ffn_w_gate": nrm(ks[12], (DEPTH, D_MODEL, D_FF), D_MODEL ** -0.5),
        "ffn_w_up": nrm(ks[13], (DEPTH, D_MODEL, D_FF), D_MODEL ** -0.5),
        "ffn_conv_w": nrm(ks[14], (DEPTH, CONV_W, D_FF), CONV_W ** -0.5),
        "ffn_conv_b": nrm(ks[15], (DEPTH, D_FF), 0.02),
        "ffn_w_down": nrm(ks[16], (DEPTH, D_FF, D_MODEL), D_FF ** -0.5),
        "ple_norm_g": gain(ks[17], (DEPTH, D_MODEL)),
        "ple_w_gate": nrm(ks[18], (DEPTH, D_MODEL, D_MODEL), D_MODEL ** -0.5),
        "ple_w_proj": nrm(ks[19], (DEPTH, PLE_DIM, D_MODEL), PLE_DIM ** -0.5),
        "final_norm_g": gain(ks[20], (D_MODEL,)),
    }


def reference(x, p, mix_norm_g, w_in, sgu_norm_g, sgu_w, sgu_b, ret_norm_g, w_branch_a, w_branch_b,
              w_out, ffn_norm_g, ffn_w_gate, ffn_w_up, ffn_conv_w, ffn_conv_b, ffn_w_down,
              ple_norm_g, ple_w_gate, ple_w_proj, final_norm_g):
    B, S, _ = x.shape
    pos = jnp.arange(S, dtype=jnp.int32)
    for i in range(DEPTH):
        h = rms_norm(x, mix_norm_g[i])
        z = h @ w_in[i]
        u, v, q, k, rv, rg, ga, gb = jnp.split(z, IN_OFFSETS, axis=-1)
        ya = spatial_gating(jax.nn.gelu(u), jax.nn.gelu(v), sgu_norm_g[i], sgu_w[i], sgu_b[i]) @ w_branch_a[i]
        q = rotary(q.reshape(B, S, RET_HEADS, RET_DK), pos)
        k = rotary(k.reshape(B, S, RET_HEADS, RET_DK), pos) * (RET_DK ** -0.5)
        y = retention(q, k, rv.reshape(B, S, RET_HEADS, RET_DV))
        y = rms_norm(y, ret_norm_g[i].reshape(RET_HEADS, RET_DV)).reshape(B, S, RET_V_WIDTH)
        yb = (jax.nn.silu(rg) * y) @ w_branch_b[i]
        m = jax.nn.sigmoid(ga) * ya + jax.nn.sigmoid(gb) * yb
        x = x + m @ w_out[i]
        h = rms_norm(x, ffn_norm_g[i])
        a = causal_dwconv(h @ ffn_w_gate[i], ffn_conv_w[i], ffn_conv_b[i])
        x = x + (jax.nn.gelu(a) * (h @ ffn_w_up[i])) @ ffn_w_down[i]
        g = jax.nn.sigmoid(rms_norm(x, ple_norm_g[i]) @ ple_w_gate[i])
        x = x + (p[i] @ ple_w_proj[i]) * g
    return rms_norm(x, final_norm_g)
```

```python
import functools

import jax
import jax.numpy as jnp
from jax import lax
from jax.experimental import pallas as pl
from jax.experimental.pallas import tpu as pltpu

D_MODEL = 2048
CHUNK = 128
SGU_GROUPS = 8
SGU_GROUP_DIM = 128
SGU_WIDTH = SGU_GROUPS * SGU_GROUP_DIM
RET_HEADS = 8
RET_DK = 128
RET_DV = 256
RET_QK_WIDTH = RET_HEADS * RET_DK
RET_V_WIDTH = RET_HEADS * RET_DV
ROPE_THETA = 10000.0
D_FF = 5632
CONV_W = 3
PLE_DIM = 256
NORM_EPS = 1e-6
IN_COLS = 2 * SGU_WIDTH + 2 * RET_QK_WIDTH + 2 * RET_V_WIDTH + 2 * D_MODEL

V7X_VMEM_LIMIT_BYTES = 56000 * 1024
SUBLANES = 8

BF16 = jnp.bfloat16
F32 = jnp.float32

IN_BLOCK = 1024
BLK_U, BLK_V, BLK_Q, BLK_K = 0, 1, 2, 3
BLK_RV, BLK_RG, BLK_GA, BLK_GB = 4, 6, 8, 10


def _cparams(n_axes):
    return pltpu.CompilerParams(
        dimension_semantics=("arbitrary",) * n_axes,
        vmem_limit_bytes=V7X_VMEM_LIMIT_BYTES,
    )


def _rms_rows(x, g):
    y = x * lax.rsqrt(jnp.mean(x * x, axis=-1, keepdims=True) + NORM_EPS)
    return y * g


def _dot(a, b):
    return jnp.dot(a, b, preferred_element_type=F32)


def _in_proj_kernel(x_ref, g_ref, w_ref, cos_ref, sin_ref, o_ref, h_ref):
    j = pl.program_id(1)

    @pl.when(j == 0)
    def _():
        h_ref[...] = _rms_rows(x_ref[...], g_ref[...]).astype(BF16)

    z = _dot(h_ref[...], w_ref[...])

    def rotary(scale):
        cos = cos_ref[...]
        sin = sin_ref[...]
        for h in range(RET_HEADS):
            sl = slice(h * RET_DK, (h + 1) * RET_DK)
            zh = z[:, sl]
            r = zh * cos + pltpu.roll(zh, RET_DK // 2, 1) * sin
            if scale is not None:
                r = r * scale
            o_ref[:, sl] = r.astype(o_ref.dtype)

    @pl.when(j < BLK_Q)
    def _():
        o_ref[...] = jax.nn.gelu(z).astype(o_ref.dtype)

    @pl.when(j == BLK_Q)
    def _():
        rotary(None)

    @pl.when(j == BLK_K)
    def _():
        rotary(RET_DK ** -0.5)

    @pl.when((j >= BLK_RV) & (j < BLK_RG))
    def _():
        o_ref[...] = z.astype(o_ref.dtype)

    @pl.when((j >= BLK_RG) & (j < BLK_GA))
    def _():
        o_ref[...] = jax.nn.silu(z).astype(o_ref.dtype)

    @pl.when(j >= BLK_GA)
    def _():
        o_ref[...] = jax.nn.sigmoid(z).astype(o_ref.dtype)


def _in_proj(x2, g, w, cos_t, sin_t, seq, tm=1024):
    m, d = x2.shape
    n = w.shape[1]
    pos_blocks = seq // tm
    return pl.pallas_call(
        _in_proj_kernel,
        grid=(m // tm, n // IN_BLOCK),
        in_specs=[
            pl.BlockSpec((tm, d), lambda i, j: (i, 0)),
            pl.BlockSpec((1, d), lambda i, j: (0, 0)),
            pl.BlockSpec((d, IN_BLOCK), lambda i, j: (0, j)),
            pl.BlockSpec((tm, RET_DK), lambda i, j: (i % pos_blocks, 0)),
            pl.BlockSpec((tm, RET_DK), lambda i, j: (i % pos_blocks, 0)),
        ],
        out_specs=pl.BlockSpec((tm, IN_BLOCK), lambda i, j: (i, j)),
        out_shape=jax.ShapeDtypeStruct((m, n), BF16),
        scratch_shapes=[pltpu.VMEM((tm, d), BF16)],
        compiler_params=_cparams(2),
        name="in_proj",
    )(x2, g, w, cos_t, sin_t)


def _mixer_kernel(gu_ref, gv_ref, q_ref, k_ref, rv_ref, srg_ref,
                  sgu_g_ref, sgu_w_ref, sgu_bias_ref, ret_g_ref,
                  decay_ref, xi_ref, zeta_ref, cd_ref,
                  a_ref, b_ref, state_ref):
    n = pl.program_id(1)

    @pl.when(n == 0)
    def _():
        state_ref[...] = jnp.zeros_like(state_ref)

    vn = _rms_rows(gv_ref[...].astype(F32), sgu_g_ref[...]).astype(BF16)
    row = lax.broadcasted_iota(jnp.int32, (CHUNK, CHUNK), 0)
    col = lax.broadcasted_iota(jnp.int32, (CHUNK, CHUNK), 1)
    causal = row >= col
    for g in range(SGU_GROUPS):
        sl = slice(g * SGU_GROUP_DIM, (g + 1) * SGU_GROUP_DIM)
        w = jnp.where(causal, sgu_w_ref[g], 0.0).astype(BF16)
        s = _dot(w, vn[:, sl]) + sgu_bias_ref[:, sl]
        a_ref[:, sl] = (gu_ref[:, sl].astype(F32) * s).astype(a_ref.dtype)

    for h in range(RET_HEADS):
        ksl = slice(h * RET_DK, (h + 1) * RET_DK)
        vsl = slice(h * RET_DV, (h + 1) * RET_DV)
        q = q_ref[:, ksl]
        k = k_ref[:, ksl]
        v = rv_ref[:, vsl]
        scores = lax.dot_general(q, k, (((1,), (1,)), ((), ())), preferred_element_type=F32)
        scores = (scores * decay_ref[h]).astype(BF16)
        inner = _dot(scores, v)
        prev = state_ref[h]
        cross = _dot(q, prev.astype(BF16)) * xi_ref[:, vsl]
        kz = (k.astype(F32) * zeta_ref[:, ksl]).astype(BF16)
        kv = lax.dot_general(kz, v, (((0,), (0,)), ((), ())), preferred_element_type=F32)
        state_ref[h] = prev * cd_ref[h] + kv
        y = _rms_rows(inner + cross, ret_g_ref[:, vsl])
        b_ref[:, vsl] = (srg_ref[:, vsl].astype(F32) * y).astype(b_ref.dtype)


def _mixer(z, sgu_g, sgu_w, sgu_bias, ret_g, decay, xi_map, zeta_map, cd, batch, seq):
    m = z.shape[0]
    n_chunks = seq // CHUNK
    rv_blk = BLK_RV * IN_BLOCK // RET_V_WIDTH
    rg_blk = BLK_RG * IN_BLOCK // RET_V_WIDTH

    def zspec(width, blk):
        return pl.BlockSpec((CHUNK, width), lambda b, n: (b * n_chunks + n, blk))

    def const(shape):
        return pl.BlockSpec(shape, lambda b, n: (0,) * len(shape))

    return pl.pallas_call(
        _mixer_kernel,
        grid=(batch, n_chunks),
        in_specs=[
            zspec(SGU_WIDTH, BLK_U), zspec(SGU_WIDTH, BLK_V),
            zspec(RET_QK_WIDTH, BLK_Q), zspec(RET_QK_WIDTH, BLK_K),
            zspec(RET_V_WIDTH, rv_blk), zspec(RET_V_WIDTH, rg_blk),
            const((1, SGU_WIDTH)), const((SGU_GROUPS, CHUNK, CHUNK)), const((CHUNK, SGU_WIDTH)),
            const((1, RET_V_WIDTH)),
            const((RET_HEADS, CHUNK, CHUNK)), const((CHUNK, RET_V_WIDTH)), const((CHUNK, RET_QK_WIDTH)),
            const((RET_HEADS, 1, RET_DV)),
        ],
        out_specs=[
            pl.BlockSpec((CHUNK, SGU_WIDTH), lambda b, n: (b * n_chunks + n, 0)),
            pl.BlockSpec((CHUNK, RET_V_WIDTH), lambda b, n: (b * n_chunks + n, 0)),
        ],
        out_shape=[
            jax.ShapeDtypeStruct((m, SGU_WIDTH), BF16),
            jax.ShapeDtypeStruct((m, RET_V_WIDTH), BF16),
        ],
        scratch_shapes=[pltpu.VMEM((RET_HEADS, RET_DK, RET_DV), F32)],
        compiler_params=_cparams(2),
        name="mixer",
    )(z, z, z, z, z, z, sgu_g, sgu_w, sgu_bias, ret_g, decay, xi_map, zeta_map, cd)


def _merge_kernel(a_ref, b_ref, wa_ref, wb_ref, sga_ref, sgb_ref, o_ref):
    ya = _dot(a_ref[...], wa_ref[...])
    yb = _dot(b_ref[...], wb_ref[...])
    m = sga_ref[...].astype(F32) * ya + sgb_ref[...].astype(F32) * yb
    o_ref[...] = m.astype(o_ref.dtype)


def _merge(a, b, wa, wb, z, tm=1024, tn=1024):
    m = a.shape[0]
    n = wa.shape[1]
    ga_blk = BLK_GA * IN_BLOCK // tn
    gb_blk = BLK_GB * IN_BLOCK // tn
    return pl.pallas_call(
        _merge_kernel,
        grid=(m // tm, n // tn),
        in_specs=[
            pl.BlockSpec((tm, a.shape[1]), lambda i, j: (i, 0)),
            pl.BlockSpec((tm, b.shape[1]), lambda i, j: (i, 0)),
            pl.BlockSpec((wa.shape[0], tn), lambda i, j: (0, j)),
            pl.BlockSpec((wb.shape[0], tn), lambda i, j: (0, j)),
            pl.BlockSpec((tm, tn), lambda i, j: (i, ga_blk + j)),
            pl.BlockSpec((tm, tn), lambda i, j: (i, gb_blk + j)),
        ],
        out_specs=pl.BlockSpec((tm, tn), lambda i, j: (i, j)),
        out_shape=jax.ShapeDtypeStruct((m, n), BF16),
        compiler_params=_cparams(2),
        name="merge",
    )(a, b, wa, wb, z, z)


def _resid_proj_kernel(a_ref, w_ref, x_ref, o_ref):
    o_ref[...] = x_ref[...] + _dot(a_ref[...], w_ref[...])


def _resid_proj(a, w, x2, tm, tn, name):
    m, k = a.shape
    n = w.shape[1]
    return pl.pallas_call(
        _resid_proj_kernel,
        grid=(m // tm, n // tn),
        in_specs=[
            pl.BlockSpec((tm, k), lambda i, j: (i, 0)),
            pl.BlockSpec((k, tn), lambda i, j: (0, j)),
            pl.BlockSpec((tm, tn), lambda i, j: (i, j)),
        ],
        out_specs=pl.BlockSpec((tm, tn), lambda i, j: (i, j)),
        out_shape=jax.ShapeDtypeStruct((m, n), F32),
        compiler_params=_cparams(2),
        name=name,
    )(a, w, x2)


def _ffn_up_kernel(x_ref, halo_ref, g_ref, wg_ref, wu_ref, cw_ref, cb_ref, o_ref, h_ref,
                   *, blocks_per_seq):
    i = pl.program_id(0)
    j = pl.program_id(1)
    tm = x_ref.shape[0]

    @pl.when(j == 0)
    def _():
        keep = (i % blocks_per_seq != 0).astype(F32)
        h_ref[:SUBLANES, :] = (_rms_rows(halo_ref[...], g_ref[...]) * keep).astype(BF16)
        h_ref[SUBLANES:, :] = _rms_rows(x_ref[...], g_ref[...]).astype(BF16)

    gate = _dot(h_ref[...], wg_ref[...])
    up = _dot(h_ref[SUBLANES:, :], wu_ref[...])
    g_t = gate[SUBLANES:, :]
    g_t1 = pltpu.roll(gate, 1, 0)[SUBLANES:, :]
    g_t2 = pltpu.roll(gate, 2, 0)[SUBLANES:, :]
    a = g_t2 * cw_ref[0:1, :]
    a = a + g_t1 * cw_ref[1:2, :]
    a = a + g_t * cw_ref[2:3, :]
    a = a + cb_ref[...]
    o_ref[...] = (jax.nn.gelu(a) * up).astype(o_ref.dtype)


def _ffn_up(x2, g, wg, wu, cw, cb, seq, tm=1024, tf=512):
    m, d = x2.shape
    f = wg.shape[1]
    halo_blocks = tm // SUBLANES
    kern = functools.partial(_ffn_up_kernel, blocks_per_seq=seq // tm)
    return pl.pallas_call(
        kern,
        grid=(m // tm, f // tf),
        in_specs=[
            pl.BlockSpec((tm, d), lambda i, j: (i, 0)),
            pl.BlockSpec((SUBLANES, d), lambda i, j: (jnp.maximum(i * halo_blocks - 1, 0), 0)),
            pl.BlockSpec((1, d), lambda i, j: (0, 0)),
            pl.BlockSpec((d, tf), lambda i, j: (0, j)),
            pl.BlockSpec((d, tf), lambda i, j: (0, j)),
            pl.BlockSpec((CONV_W, tf), lambda i, j: (0, j)),
            pl.BlockSpec((1, tf), lambda i, j: (0, j)),
        ],
        out_specs=pl.BlockSpec((tm, tf), lambda i, j: (i, j)),
        out_shape=jax.ShapeDtypeStruct((m, f), BF16),
        scratch_shapes=[pltpu.VMEM((SUBLANES + tm, d), BF16)],
        compiler_params=_cparams(2),
        name="ffn_up",
    )(x2, x2, g, wg, wu, cw, cb)


def _ple_kernel(xf_ref, xt_ref, p_ref, g_ref, wg_ref, wp_ref, o_ref, h_ref):
    j = pl.program_id(1)

    @pl.when(j == 0)
    def _():
        h_ref[...] = _rms_rows(xf_ref[...], g_ref[...]).astype(BF16)

    gate = jax.nn.sigmoid(_dot(h_ref[...], wg_ref[...]))
    proj = _dot(p_ref[...].astype(BF16), wp_ref[...])
    o_ref[...] = xt_ref[...] + proj * gate


def _ple(x2, p2, g, wg, wp, tm=1024, tn=1024):
    m, d = x2.shape
    return pl.pallas_call(
        _ple_kernel,
        grid=(m // tm, d // tn),
        in_specs=[
            pl.BlockSpec((tm, d), lambda i, j: (i, 0)),
            pl.BlockSpec((tm, tn), lambda i, j: (i, j)),
            pl.BlockSpec((tm, PLE_DIM), lambda i, j: (i, 0)),
            pl.BlockSpec((1, d), lambda i, j: (0, 0)),
            pl.BlockSpec((d, tn), lambda i, j: (0, j)),
            pl.BlockSpec((PLE_DIM, tn), lambda i, j: (0, j)),
        ],
        out_specs=pl.BlockSpec((tm, tn), lambda i, j: (i, j)),
        out_shape=jax.ShapeDtypeStruct((m, d), F32),
        scratch_shapes=[pltpu.VMEM((tm, d), BF16)],
        compiler_params=_cparams(2),
        name="ple",
    )(x2, x2, p2, g, wg, wp)


def _final_norm_kernel(x_ref, g_ref, o_ref):
    o_ref[...] = _rms_rows(x_ref[...], g_ref[...])


def _final_norm(x2, g, tm=512):
    m, d = x2.shape
    return pl.pallas_call(
        _final_norm_kernel,
        grid=(m // tm,),
        in_specs=[pl.BlockSpec((tm, d), lambda i: (i, 0)), pl.BlockSpec((1, d), lambda i: (0, 0))],
        out_specs=pl.BlockSpec((tm, d), lambda i: (i, 0)),
        out_shape=jax.ShapeDtypeStruct((m, d), F32),
        compiler_params=_cparams(1),
        name="final_norm",
    )(x2, g)


def _rotary_tables(seq):
    half = RET_DK // 2
    inv = ROPE_THETA ** (-jnp.arange(half, dtype=F32) / half)
    ang = jnp.arange(seq, dtype=jnp.int32).astype(F32)[:, None] * inv[None, :]
    cos = jnp.cos(ang)
    sin = jnp.sin(ang)
    return jnp.concatenate([cos, cos], axis=-1), jnp.concatenate([-sin, sin], axis=-1)


def _retention_tables():
    log_g = jnp.log1p(-jnp.exp2(-5.0 - jnp.arange(RET_HEADS, dtype=F32)))
    idx = jnp.arange(CHUNK, dtype=F32)
    diff = idx[:, None] - idx[None, :]
    causal = diff >= 0
    decay = jnp.where(causal[None], jnp.exp(log_g[:, None, None] * jnp.where(causal, diff, 0.0)[None]), 0.0)
    xi = jnp.exp(log_g[:, None] * (idx[None, :] + 1.0)).T
    zeta = jnp.exp(log_g[:, None] * (CHUNK - 1.0 - idx[None, :])).T
    chunk_decay = jnp.exp(log_g * CHUNK)
    xi_map = jnp.repeat(xi, RET_DV, axis=1)
    zeta_map = jnp.repeat(zeta, RET_DK, axis=1)
    cd = jnp.broadcast_to(chunk_decay[:, None, None], (RET_HEADS, 1, RET_DV))
    return decay, xi_map, zeta_map, cd


def kernel(x, p, mix_norm_g, w_in, sgu_norm_g, sgu_w, sgu_b, ret_norm_g, w_branch_a, w_branch_b, w_out, ffn_norm_g, ffn_w_gate, ffn_w_up, ffn_conv_w, ffn_conv_b, ffn_w_down, ple_norm_g, ple_w_gate, ple_w_proj, final_norm_g):
    batch, seq, d = x.shape
    depth = w_in.shape[0]
    m = batch * seq
    assert d == D_MODEL and seq % 1024 == 0 and w_in.shape[2] == IN_COLS

    cos_t, sin_t = _rotary_tables(seq)
    decay, xi_map, zeta_map, cd = _retention_tables()

    x2 = x.reshape(m, d)
    for i in range(depth):
        z = _in_proj(x2, mix_norm_g[i][None], w_in[i].astype(BF16), cos_t, sin_t, seq)
        sgu_bias = jnp.repeat(sgu_b[i].T, SGU_GROUP_DIM, axis=1)
        a, b = _mixer(z, sgu_norm_g[i][None], sgu_w[i], sgu_bias, ret_norm_g[i][None],
                      decay, xi_map, zeta_map, cd, batch, seq)
        mrg = _merge(a, b, w_branch_a[i].astype(BF16), w_branch_b[i].astype(BF16), z)
        x2 = _resid_proj(mrg, w_out[i].astype(BF16), x2, 1024, 1024, "out_proj")
        act = _ffn_up(x2, ffn_norm_g[i][None], ffn_w_gate[i].astype(BF16), ffn_w_up[i].astype(BF16),
                      ffn_conv_w[i], ffn_conv_b[i][None], seq)
        x2 = _resid_proj(act, ffn_w_down[i].astype(BF16), x2, 1024, 512, "ffn_down")
        x2 = _ple(x2, p[i].reshape(m, PLE_DIM), ple_norm_g[i][None], ple_w_gate[i].astype(BF16),
                  ple_w_proj[i].astype(BF16))
    return _final_norm(x2, final_norm_g[None]).reshape(batch, seq, d)
```

```python
import functools

import jax
import jax.numpy as jnp
from jax import lax
from jax.experimental import pallas as pl
from jax.experimental.pallas import tpu as pltpu

D_MODEL = 2048
CHUNK = 128
SGU_GROUPS = 8
SGU_GROUP_DIM = 128
SGU_WIDTH = SGU_GROUPS * SGU_GROUP_DIM
RET_HEADS = 8
RET_DK = 128
RET_DV = 256
RET_QK_WIDTH = RET_HEADS * RET_DK
RET_V_WIDTH = RET_HEADS * RET_DV
ROPE_THETA = 10000.0
D_FF = 5632
CONV_W = 3
PLE_DIM = 256
NORM_EPS = 1e-6
IN_COLS = 2 * SGU_WIDTH + 2 * RET_QK_WIDTH + 2 * RET_V_WIDTH + 2 * D_MODEL

V7X_VMEM_LIMIT_BYTES = 56000 * 1024
BF16_ROWS = 16

BF16 = jnp.bfloat16
F32 = jnp.float32

COL_UV = 0
COL_QK = 2 * SGU_WIDTH
COL_RV = COL_QK + 2 * RET_QK_WIDTH
COL_RG = COL_RV + RET_V_WIDTH
COL_G = COL_RG + RET_V_WIDTH


def _cparams(n_axes):
    return pltpu.CompilerParams(
        dimension_semantics=("arbitrary",) * n_axes,
        vmem_limit_bytes=V7X_VMEM_LIMIT_BYTES,
    )


def _rms_rows(x, g):
    y = x * lax.rsqrt(jnp.mean(x * x, axis=-1, keepdims=True) + NORM_EPS)
    return y * g


def _dot(a, b):
    return jnp.dot(a, b, preferred_element_type=F32)


def _norm_kernel(x_ref, g_ref, o_ref):
    o_ref[...] = _rms_rows(x_ref[...], g_ref[...]).astype(o_ref.dtype)


def _norm_bf16(x2, g, tm=512):
    m, d = x2.shape
    return pl.pallas_call(
        _norm_kernel,
        grid=(m // tm,),
        in_specs=[pl.BlockSpec((tm, d), lambda i: (i, 0)), pl.BlockSpec((1, d), lambda i: (0, 0))],
        out_specs=pl.BlockSpec((tm, d), lambda i: (i, 0)),
        out_shape=jax.ShapeDtypeStruct((m, d), BF16),
        compiler_params=_cparams(1),
        name="norm",
    )(x2, g)


def _norm_halo_kernel(x_ref, halo_ref, g_ref, o_ref, *, blocks_per_seq):
    keep = (pl.program_id(0) % blocks_per_seq != 0).astype(F32)
    o_ref[:BF16_ROWS, :] = (_rms_rows(halo_ref[...], g_ref[...]) * keep).astype(o_ref.dtype)
    o_ref[BF16_ROWS:, :] = _rms_rows(x_ref[...], g_ref[...]).astype(o_ref.dtype)


def _norm_halo_bf16(x2, g, seq, tm):
    m, d = x2.shape
    halo_blocks = tm // BF16_ROWS
    kern = functools.partial(_norm_halo_kernel, blocks_per_seq=seq // tm)
    return pl.pallas_call(
        kern,
        grid=(m // tm,),
        in_specs=[
            pl.BlockSpec((tm, d), lambda i: (i, 0)),
            pl.BlockSpec((BF16_ROWS, d), lambda i: (jnp.maximum(i * halo_blocks - 1, 0), 0)),
            pl.BlockSpec((1, d), lambda i: (0, 0)),
        ],
        out_specs=pl.BlockSpec((None, BF16_ROWS + tm, d), lambda i: (i, 0, 0)),
        out_shape=jax.ShapeDtypeStruct((m // tm, BF16_ROWS + tm, d), BF16),
        compiler_params=_cparams(1),
        name="norm_halo",
    )(x2, x2, g)


def _proj_kernel(h_ref, w_ref, o_ref, wb_ref, *, act):
    @pl.when(pl.program_id(1) == 0)
    def _():
        wb_ref[...] = w_ref[...].astype(BF16)

    o_ref[...] = act(_dot(h_ref[...], wb_ref[...])).astype(o_ref.dtype)


def _rot_kernel(h_ref, w_ref, cos_ref, sin_ref, scale_ref, o_ref, wb_ref):
    @pl.when(pl.program_id(1) == 0)
    def _():
        wb_ref[...] = w_ref[...].astype(BF16)

    z = _dot(h_ref[...], wb_ref[...])
    cos = cos_ref[...]
    sin = sin_ref[...]
    for h in range(z.shape[1] // RET_DK):
        sl = slice(h * RET_DK, (h + 1) * RET_DK)
        zh = z[:, sl]
        r = zh * cos + pltpu.roll(zh, RET_DK // 2, 1) * sin
        o_ref[:, sl] = (r * scale_ref[:, sl]).astype(o_ref.dtype)


def _in_proj(h, w, layer, col0, n, act, tm=1024, tn=1024):
    m, d = h.shape
    off = col0 // tn
    return pl.pallas_call(
        functools.partial(_proj_kernel, act=act),
        grid=(n // tn, m // tm),
        in_specs=[
            pl.BlockSpec((tm, d), lambda j, i: (i, 0)),
            pl.BlockSpec((None, d, tn), lambda j, i: (layer, 0, off + j)),
        ],
        out_specs=pl.BlockSpec((tm, tn), lambda j, i: (i, j)),
        out_shape=jax.ShapeDtypeStruct((m, n), BF16),
        scratch_shapes=[pltpu.VMEM((d, tn), BF16)],
        compiler_params=_cparams(2),
        name="in_proj",
    )(h, w)


def _in_proj_rot(h, w, layer, col0, n, cos_t, sin_t, scale, seq, tm=1024, tn=1024):
    m, d = h.shape
    off = col0 // tn
    pos_blocks = seq // tm
    return pl.pallas_call(
        _rot_kernel,
        grid=(n // tn, m // tm),
        in_specs=[
            pl.BlockSpec((tm, d), lambda j, i: (i, 0)),
            pl.BlockSpec((None, d, tn), lambda j, i: (layer, 0, off + j)),
            pl.BlockSpec((tm, RET_DK), lambda j, i: (i % pos_blocks, 0)),
            pl.BlockSpec((tm, RET_DK), lambda j, i: (i % pos_blocks, 0)),
            pl.BlockSpec((1, tn), lambda j, i: (0, j)),
        ],
        out_specs=pl.BlockSpec((tm, tn), lambda j, i: (i, j)),
        out_shape=jax.ShapeDtypeStruct((m, n), BF16),
        scratch_shapes=[pltpu.VMEM((d, tn), BF16)],
        compiler_params=_cparams(2),
        name="in_proj_rot",
    )(h, w, cos_t, sin_t, scale)


def _mixer_kernel(gu_ref, gv_ref, q_ref, k_ref, rv_ref, srg_ref,
                  sgu_g_ref, sgu_w_ref, sgu_bias_ref, ret_g_ref,
                  decay_ref, xi_ref, zeta_ref, cd_ref,
                  a_ref, b_ref, state_ref):
    n = pl.program_id(1)

    @pl.when(n == 0)
    def _():
        state_ref[...] = jnp.zeros_like(state_ref)

    vn = _rms_rows(gv_ref[...].astype(F32), sgu_g_ref[...]).astype(BF16)
    row = lax.broadcasted_iota(jnp.int32, (CHUNK, CHUNK), 0)
    col = lax.broadcasted_iota(jnp.int32, (CHUNK, CHUNK), 1)
    causal = row >= col
    for g in range(SGU_GROUPS):
        sl = slice(g * SGU_GROUP_DIM, (g + 1) * SGU_GROUP_DIM)
        w = jnp.where(causal, sgu_w_ref[g], 0.0).astype(BF16)
        s = _dot(w, vn[:, sl]) + sgu_bias_ref[:, sl]
        a_ref[:, sl] = (gu_ref[:, sl].astype(F32) * s).astype(a_ref.dtype)

    for h in range(RET_HEADS):
        ksl = slice(h * RET_DK, (h + 1) * RET_DK)
        vsl = slice(h * RET_DV, (h + 1) * RET_DV)
        q = q_ref[:, ksl]
        k = k_ref[:, ksl]
        v = rv_ref[:, vsl]
        scores = lax.dot_general(q, k, (((1,), (1,)), ((), ())), preferred_element_type=F32)
        scores = (scores * decay_ref[h]).astype(BF16)
        inner = _dot(scores, v)
        prev = state_ref[h]
        cross = _dot(q, prev.astype(BF16)) * xi_ref[:, vsl]
        kz = (k.astype(F32) * zeta_ref[:, ksl]).astype(BF16)
        kv = lax.dot_general(kz, v, (((0,), (0,)), ((), ())), preferred_element_type=F32)
        state_ref[h] = prev * cd_ref[h] + kv
        y = _rms_rows(inner + cross, ret_g_ref[:, vsl])
        b_ref[:, vsl] = (srg_ref[:, vsl].astype(F32) * y).astype(b_ref.dtype)


def _mixer(uv, qk, rv, srg, sgu_g, sgu_w, sgu_bias, ret_g, decay, xi_map, zeta_map, cd, batch, seq):
    m = uv.shape[0]
    n_chunks = seq // CHUNK

    def rows(width, blk):
        return pl.BlockSpec((CHUNK, width), lambda b, n: (b * n_chunks + n, blk))

    def const(shape):
        return pl.BlockSpec(shape, lambda b, n: (0,) * len(shape))

    return pl.pallas_call(
        _mixer_kernel,
        grid=(batch, n_chunks),
        in_specs=[
            rows(SGU_WIDTH, 0), rows(SGU_WIDTH, 1),
            rows(RET_QK_WIDTH, 0), rows(RET_QK_WIDTH, 1),
            rows(RET_V_WIDTH, 0), rows(RET_V_WIDTH, 0),
            const((1, SGU_WIDTH)), const((SGU_GROUPS, CHUNK, CHUNK)), const((CHUNK, SGU_WIDTH)),
            const((1, RET_V_WIDTH)),
            const((RET_HEADS, CHUNK, CHUNK)), const((CHUNK, RET_V_WIDTH)), const((CHUNK, RET_QK_WIDTH)),
            const((RET_HEADS, 1, RET_DV)),
        ],
        out_specs=[rows(SGU_WIDTH, 0), rows(RET_V_WIDTH, 0)],
        out_shape=[
            jax.ShapeDtypeStruct((m, SGU_WIDTH), BF16),
            jax.ShapeDtypeStruct((m, RET_V_WIDTH), BF16),
        ],
        scratch_shapes=[pltpu.VMEM((RET_HEADS, RET_DK, RET_DV), F32)],
        compiler_params=_cparams(2),
        name="mixer",
    )(uv, uv, qk, qk, rv, srg, sgu_g, sgu_w, sgu_bias, ret_g, decay, xi_map, zeta_map, cd)


def _merge_kernel(a_ref, b_ref, wa_ref, wb_ref, sga_ref, sgb_ref, o_ref):
    ya = _dot(a_ref[...], wa_ref[...])
    yb = _dot(b_ref[...], wb_ref[...])
    m = sga_ref[...].astype(F32) * ya + sgb_ref[...].astype(F32) * yb
    o_ref[...] = m.astype(o_ref.dtype)


def _merge(a, b, wa, wb, sg, tm=1024, tn=1024):
    m = a.shape[0]
    n = wa.shape[1]
    gb_blk = n // tn
    return pl.pallas_call(
        _merge_kernel,
        grid=(m // tm, n // tn),
        in_specs=[
            pl.BlockSpec((tm, a.shape[1]), lambda i, j: (i, 0)),
            pl.BlockSpec((tm, b.shape[1]), lambda i, j: (i, 0)),
            pl.BlockSpec((wa.shape[0], tn), lambda i, j: (0, j)),
            pl.BlockSpec((wb.shape[0], tn), lambda i, j: (0, j)),
            pl.BlockSpec((tm, tn), lambda i, j: (i, j)),
            pl.BlockSpec((tm, tn), lambda i, j: (i, gb_blk + j)),
        ],
        out_specs=pl.BlockSpec((tm, tn), lambda i, j: (i, j)),
        out_shape=jax.ShapeDtypeStruct((m, n), BF16),
        compiler_params=_cparams(2),
        name="merge",
    )(a, b, wa, wb, sg, sg)


def _resid_proj_kernel(a_ref, w_ref, x_ref, o_ref):
    o_ref[...] = x_ref[...] + _dot(a_ref[...], w_ref[...])


def _resid_proj(a, w, x2, tm, tn, name):
    m, k = a.shape
    n = w.shape[1]
    return pl.pallas_call(
        _resid_proj_kernel,
        grid=(m // tm, n // tn),
        in_specs=[
            pl.BlockSpec((tm, k), lambda i, j: (i, 0)),
            pl.BlockSpec((k, tn), lambda i, j: (0, j)),
            pl.BlockSpec((tm, tn), lambda i, j: (i, j)),
        ],
        out_specs=pl.BlockSpec((tm, tn), lambda i, j: (i, j)),
        out_shape=jax.ShapeDtypeStruct((m, n), F32),
        compiler_params=_cparams(2),
        name=name,
    )(a, w, x2)


def _ffn_up_kernel(h_ref, wg_ref, wu_ref, cw_ref, cb_ref, o_ref, wgb_ref, wub_ref):
    @pl.when(pl.program_id(1) == 0)
    def _():
        wgb_ref[...] = wg_ref[...].astype(BF16)
        wub_ref[...] = wu_ref[...].astype(BF16)

    gate = _dot(h_ref[...], wgb_ref[...])
    up = _dot(h_ref[BF16_ROWS:, :], wub_ref[...])
    g_t = gate[BF16_ROWS:, :]
    g_t1 = pltpu.roll(gate, 1, 0)[BF16_ROWS:, :]
    g_t2 = pltpu.roll(gate, 2, 0)[BF16_ROWS:, :]
    a = g_t2 * cw_ref[0:1, :]
    a = a + g_t1 * cw_ref[1:2, :]
    a = a + g_t * cw_ref[2:3, :]
    a = a + cb_ref[...]
    o_ref[...] = (jax.nn.gelu(a) * up).astype(o_ref.dtype)


def _ffn_up(h_ext, wg, wu, layer, cw, cb, tf=512):
    nblk, rows, d = h_ext.shape
    tm = rows - BF16_ROWS
    f = wg.shape[2]
    return pl.pallas_call(
        _ffn_up_kernel,
        grid=(f // tf, nblk),
        in_specs=[
            pl.BlockSpec((None, rows, d), lambda j, i: (i, 0, 0)),
            pl.BlockSpec((None, d, tf), lambda j, i: (layer, 0, j)),
            pl.BlockSpec((None, d, tf), lambda j, i: (layer, 0, j)),
            pl.BlockSpec((CONV_W, tf), lambda j, i: (0, j)),
            pl.BlockSpec((1, tf), lambda j, i: (0, j)),
        ],
        out_specs=pl.BlockSpec((tm, tf), lambda j, i: (i, j)),
        out_shape=jax.ShapeDtypeStruct((nblk * tm, f), BF16),
        scratch_shapes=[pltpu.VMEM((d, tf), BF16), pltpu.VMEM((d, tf), BF16)],
        compiler_params=_cparams(2),
        name="ffn_up",
    )(h_ext, wg, wu, cw, cb)


def _ple_kernel(xf_ref, xt_ref, p_ref, g_ref, wg_ref, wp_ref, o_ref, h_ref):
    j = pl.program_id(1)

    @pl.when(j == 0)
    def _():
        h_ref[...] = _rms_rows(xf_ref[...], g_ref[...]).astype(BF16)

    gate = jax.nn.sigmoid(_dot(h_ref[...], wg_ref[...]))
    proj = _dot(p_ref[...].astype(BF16), wp_ref[...])
    o_ref[...] = xt_ref[...] + proj * gate


def _ple(x2, p2, g, wg, wp, tm=1024, tn=1024):
    m, d = x2.shape
    return pl.pallas_call(
        _ple_kernel,
        grid=(m // tm, d // tn),
        in_specs=[
            pl.BlockSpec((tm, d), lambda i, j: (i, 0)),
            pl.BlockSpec((tm, tn), lambda i, j: (i, j)),
            pl.BlockSpec((tm, PLE_DIM), lambda i, j: (i, 0)),
            pl.BlockSpec((1, d), lambda i, j: (0, 0)),
            pl.BlockSpec((d, tn), lambda i, j: (0, j)),
            pl.BlockSpec((PLE_DIM, tn), lambda i, j: (0, j)),
        ],
        out_specs=pl.BlockSpec((tm, tn), lambda i, j: (i, j)),
        out_shape=jax.ShapeDtypeStruct((m, d), F32),
        scratch_shapes=[pltpu.VMEM((tm, d), BF16)],
        compiler_params=_cparams(2),
        name="ple",
    )(x2, x2, p2, g, wg, wp)


def _final_norm(x2, g, tm=512):
    m, d = x2.shape
    return pl.pallas_call(
        _norm_kernel,
        grid=(m // tm,),
        in_specs=[pl.BlockSpec((tm, d), lambda i: (i, 0)), pl.BlockSpec((1, d), lambda i: (0, 0))],
        out_specs=pl.BlockSpec((tm, d), lambda i: (i, 0)),
        out_shape=jax.ShapeDtypeStruct((m, d), F32),
        compiler_params=_cparams(1),
        name="final_norm",
    )(x2, g)


def _rotary_tables(seq):
    half = RET_DK // 2
    inv = ROPE_THETA ** (-jnp.arange(half, dtype=F32) / half)
    ang = jnp.arange(seq, dtype=jnp.int32).astype(F32)[:, None] * inv[None, :]
    cos = jnp.cos(ang)
    sin = jnp.sin(ang)
    return jnp.concatenate([cos, cos], axis=-1), jnp.concatenate([-sin, sin], axis=-1)


def _retention_tables():
    log_g = jnp.log1p(-jnp.exp2(-5.0 - jnp.arange(RET_HEADS, dtype=F32)))
    idx = jnp.arange(CHUNK, dtype=F32)
    diff = idx[:, None] - idx[None, :]
    causal = diff >= 0
    decay = jnp.where(causal[None], jnp.exp(log_g[:, None, None] * jnp.where(causal, diff, 0.0)[None]), 0.0)
    xi = jnp.exp(log_g[:, None] * (idx[None, :] + 1.0)).T
    zeta = jnp.exp(log_g[:, None] * (CHUNK - 1.0 - idx[None, :])).T
    chunk_decay = jnp.exp(log_g * CHUNK)
    xi_map = jnp.repeat(xi, RET_DV, axis=1)
    zeta_map = jnp.repeat(zeta, RET_DK, axis=1)
    cd = jnp.broadcast_to(chunk_decay[:, None, None], (RET_HEADS, 1, RET_DV))
    return decay, xi_map, zeta_map, cd


def kernel(x, p, mix_norm_g, w_in, sgu_norm_g, sgu_w, sgu_b, ret_norm_g, w_branch_a, w_branch_b, w_out, ffn_norm_g, ffn_w_gate, ffn_w_up, ffn_conv_w, ffn_conv_b, ffn_w_down, ple_norm_g, ple_w_gate, ple_w_proj, final_norm_g):
    batch, seq, d = x.shape
    depth = w_in.shape[0]
    m = batch * seq
    assert d == D_MODEL and seq % 1024 == 0 and w_in.shape[2] == IN_COLS

    cos_t, sin_t = _rotary_tables(seq)
    decay, xi_map, zeta_map, cd = _retention_tables()
    qk_scale = jnp.concatenate([jnp.ones((1, RET_QK_WIDTH), F32),
                                jnp.full((1, RET_QK_WIDTH), RET_DK ** -0.5, F32)], axis=1)

    x2 = x.reshape(m, d)
    for i in range(depth):
        h = _norm_bf16(x2, mix_norm_g[i][None])
        uv = _in_proj(h, w_in, i, COL_UV, 2 * SGU_WIDTH, jax.nn.gelu)
        qk = _in_proj_rot(h, w_in, i, COL_QK, 2 * RET_QK_WIDTH, cos_t, sin_t, qk_scale, seq)
        rv = _in_proj(h, w_in, i, COL_RV, RET_V_WIDTH, lambda z: z)
        srg = _in_proj(h, w_in, i, COL_RG, RET_V_WIDTH, jax.nn.silu)
        sg = _in_proj(h, w_in, i, COL_G, 2 * D_MODEL, jax.nn.sigmoid)
        sgu_bias = jnp.repeat(sgu_b[i].T, SGU_GROUP_DIM, axis=1)
        a, b = _mixer(uv, qk, rv, srg, sgu_norm_g[i][None], sgu_w[i], sgu_bias, ret_norm_g[i][None],
                      decay, xi_map, zeta_map, cd, batch, seq)
        mrg = _merge(a, b, w_branch_a[i].astype(BF16), w_branch_b[i].astype(BF16), sg)
        x2 = _resid_proj(mrg, w_out[i].astype(BF16), x2, 1024, 1024, "out_proj")
        h_ext = _norm_halo_bf16(x2, ffn_norm_g[i][None], seq, 1024)
        act = _ffn_up(h_ext, ffn_w_gate, ffn_w_up, i, ffn_conv_w[i], ffn_conv_b[i][None])
        x2 = _resid_proj(act, ffn_w_down[i].astype(BF16), x2, 1024, 512, "ffn_down")
        x2 = _ple(x2, p[i].reshape(m, PLE_DIM), ple_norm_g[i][None], ple_w_gate[i].astype(BF16),
                  ple_w_proj[i].astype(BF16))
    return _final_norm(x2, final_norm_g[None]).reshape(batch, seq, d)
```

```python
import functools

import jax
import jax.numpy as jnp
from jax import lax
from jax.experimental import pallas as pl
from jax.experimental.pallas import tpu as pltpu

D_MODEL = 2048
CHUNK = 128
SGU_GROUPS = 8
SGU_GROUP_DIM = 128
SGU_WIDTH = SGU_GROUPS * SGU_GROUP_DIM
RET_HEADS = 8
RET_DK = 128
RET_DV = 256
RET_QK_WIDTH = RET_HEADS * RET_DK
RET_V_WIDTH = RET_HEADS * RET_DV
ROPE_THETA = 10000.0
D_FF = 5632
CONV_W = 3
PLE_DIM = 256
NORM_EPS = 1e-6
IN_COLS = 2 * SGU_WIDTH + 2 * RET_QK_WIDTH + 2 * RET_V_WIDTH + 2 * D_MODEL

V7X_VMEM_LIMIT_BYTES = 56000 * 1024
SUBLANES = 8
BF16_ROWS = 16

BF16 = jnp.bfloat16
F32 = jnp.float32

COL_UV = 0
COL_QK = 2 * SGU_WIDTH
COL_RV = COL_QK + 2 * RET_QK_WIDTH
COL_RG = COL_RV + RET_V_WIDTH
COL_G = COL_RG + RET_V_WIDTH


def _cparams(n_axes):
    return pltpu.CompilerParams(
        dimension_semantics=("arbitrary",) * n_axes,
        vmem_limit_bytes=V7X_VMEM_LIMIT_BYTES,
    )


def _rms_rows(x, g):
    y = x * lax.rsqrt(jnp.mean(x * x, axis=-1, keepdims=True) + NORM_EPS)
    return y * g


def _dot(a, b):
    return jnp.dot(a, b, preferred_element_type=F32)


def _norm_kernel(x_ref, g_ref, o_ref):
    o_ref[...] = _rms_rows(x_ref[...], g_ref[...]).astype(o_ref.dtype)


def _norm_bf16(x2, g, tm=512):
    m, d = x2.shape
    return pl.pallas_call(
        _norm_kernel,
        grid=(m // tm,),
        in_specs=[pl.BlockSpec((tm, d), lambda i: (i, 0)), pl.BlockSpec((1, d), lambda i: (0, 0))],
        out_specs=pl.BlockSpec((tm, d), lambda i: (i, 0)),
        out_shape=jax.ShapeDtypeStruct((m, d), BF16),
        compiler_params=_cparams(1),
        name="norm",
    )(x2, g)


def _cast_weight_once(w_ref, wb_ref):
    @pl.when(pl.program_id(1) == 0)
    def _():
        wb_ref[...] = w_ref[...].astype(BF16)


def _cast_sides(side_refs):
    n = len(side_refs) // 2
    for src, dst in zip(side_refs[:n], side_refs[n:]):
        dst[...] = src[...].astype(dst.dtype)


def _proj_kernel(h_ref, w_ref, *rest, act, n_side):
    side_in, (o_ref,), side_out, (wb_ref,) = (rest[:n_side], rest[n_side:n_side + 1],
                                              rest[n_side + 1:2 * n_side + 1], rest[2 * n_side + 1:])
    _cast_weight_once(w_ref, wb_ref)
    _cast_sides(side_in + side_out)
    o_ref[...] = act(_dot(h_ref[...], wb_ref[...])).astype(o_ref.dtype)


def _rot_kernel(h_ref, w_ref, cos_ref, sin_ref, scale_ref, *rest, n_side):
    side_in, (o_ref,), side_out, (wb_ref,) = (rest[:n_side], rest[n_side:n_side + 1],
                                              rest[n_side + 1:2 * n_side + 1], rest[2 * n_side + 1:])
    _cast_weight_once(w_ref, wb_ref)
    _cast_sides(side_in + side_out)
    z = _dot(h_ref[...], wb_ref[...])
    cos = cos_ref[...]
    sin = sin_ref[...]
    for h in range(z.shape[1] // RET_DK):
        sl = slice(h * RET_DK, (h + 1) * RET_DK)
        zh = z[:, sl]
        r = zh * cos + pltpu.roll(zh, RET_DK // 2, 1) * sin
        o_ref[:, sl] = (r * scale_ref[:, sl]).astype(o_ref.dtype)


def _side_specs(sides, layer, n_steps, inner):
    in_specs, out_specs, out_shapes = [], [], []
    for w in sides:
        _, rows, cols = w.shape
        slab = rows // n_steps
        assert slab * n_steps == rows and slab % BF16_ROWS == 0
        in_specs.append(pl.BlockSpec((None, slab, cols), lambda j, i: (layer, j * inner + i, 0)))
        out_specs.append(pl.BlockSpec((slab, cols), lambda j, i: (j * inner + i, 0)))
        out_shapes.append(jax.ShapeDtypeStruct((rows, cols), BF16))
    return in_specs, out_specs, out_shapes


def _in_proj(h, w, layer, col0, n, act, sides=(), rot=None, tm=1024, tn=1024):
    m, d = h.shape
    off = col0 // tn
    grid = (n // tn, m // tm)
    s_in, s_out, s_shapes = _side_specs(sides, layer, grid[0] * grid[1], grid[1])
    in_specs = [
        pl.BlockSpec((tm, d), lambda j, i: (i, 0)),
        pl.BlockSpec((None, d, tn), lambda j, i: (layer, 0, off + j)),
    ]
    args = [h, w]
    if rot is None:
        kern = functools.partial(_proj_kernel, act=act, n_side=len(sides))
    else:
        cos_t, sin_t, scale, seq = rot
        pos_blocks = seq // tm
        in_specs += [
            pl.BlockSpec((tm, RET_DK), lambda j, i: (i % pos_blocks, 0)),
            pl.BlockSpec((tm, RET_DK), lambda j, i: (i % pos_blocks, 0)),
            pl.BlockSpec((1, tn), lambda j, i: (0, j)),
        ]
        args += [cos_t, sin_t, scale]
        kern = functools.partial(_rot_kernel, n_side=len(sides))
    return pl.pallas_call(
        kern,
        grid=grid,
        in_specs=in_specs + s_in,
        out_specs=[pl.BlockSpec((tm, tn), lambda j, i: (i, j))] + s_out,
        out_shape=[jax.ShapeDtypeStruct((m, n), BF16)] + s_shapes,
        scratch_shapes=[pltpu.VMEM((d, tn), BF16)],
        compiler_params=_cparams(2),
        name="in_proj_rot" if rot is not None else "in_proj",
    )(*args, *sides)


def _mixer_kernel(gu_ref, gv_ref, q_ref, k_ref, rv_ref, srg_ref,
                  sgu_g_ref, sgu_w_ref, sgu_bias_ref, ret_g_ref,
                  decay_ref, xi_ref, zeta_ref, cd_ref,
                  a_ref, b_ref, state_ref):
    n = pl.program_id(1)

    @pl.when(n == 0)
    def _():
        state_ref[...] = jnp.zeros_like(state_ref)

    vn = _rms_rows(gv_ref[...].astype(F32), sgu_g_ref[...]).astype(BF16)
    row = lax.broadcasted_iota(jnp.int32, (CHUNK, CHUNK), 0)
    col = lax.broadcasted_iota(jnp.int32, (CHUNK, CHUNK), 1)
    causal = row >= col
    for g in range(SGU_GROUPS):
        sl = slice(g * SGU_GROUP_DIM, (g + 1) * SGU_GROUP_DIM)
        w = jnp.where(causal, sgu_w_ref[g], 0.0).astype(BF16)
        s = _dot(w, vn[:, sl]) + sgu_bias_ref[:, sl]
        a_ref[:, sl] = (gu_ref[:, sl].astype(F32) * s).astype(a_ref.dtype)

    for h in range(RET_HEADS):
        ksl = slice(h * RET_DK, (h + 1) * RET_DK)
        vsl = slice(h * RET_DV, (h + 1) * RET_DV)
        q = q_ref[:, ksl]
        k = k_ref[:, ksl]
        v = rv_ref[:, vsl]
        scores = lax.dot_general(q, k, (((1,), (1,)), ((), ())), preferred_element_type=F32)
        scores = (scores * decay_ref[h]).astype(BF16)
        inner = _dot(scores, v)
        prev = state_ref[h]
        cross = _dot(q, prev.astype(BF16)) * xi_ref[:, vsl]
        kz = (k.astype(F32) * zeta_ref[:, ksl]).astype(BF16)
        kv = lax.dot_general(kz, v, (((0,), (0,)), ((), ())), preferred_element_type=F32)
        state_ref[h] = prev * cd_ref[h] + kv
        y = _rms_rows(inner + cross, ret_g_ref[:, vsl])
        b_ref[:, vsl] = (srg_ref[:, vsl].astype(F32) * y).astype(b_ref.dtype)


def _mixer(uv, qk, rv, srg, sgu_g, sgu_w, sgu_bias, ret_g, decay, xi_map, zeta_map, cd, batch, seq):
    m = uv.shape[0]
    n_chunks = seq // CHUNK

    def rows(width, blk):
        return pl.BlockSpec((CHUNK, width), lambda b, n: (b * n_chunks + n, blk))

    def const(shape):
        return pl.BlockSpec(shape, lambda b, n: (0,) * len(shape))

    return pl.pallas_call(
        _mixer_kernel,
        grid=(batch, n_chunks),
        in_specs=[
            rows(SGU_WIDTH, 0), rows(SGU_WIDTH, 1),
            rows(RET_QK_WIDTH, 0), rows(RET_QK_WIDTH, 1),
            rows(RET_V_WIDTH, 0), rows(RET_V_WIDTH, 0),
            const((1, SGU_WIDTH)), const((SGU_GROUPS, CHUNK, CHUNK)), const((CHUNK, SGU_WIDTH)),
            const((1, RET_V_WIDTH)),
            const((RET_HEADS, CHUNK, CHUNK)), const((CHUNK, RET_V_WIDTH)), const((CHUNK, RET_QK_WIDTH)),
            const((RET_HEADS, 1, RET_DV)),
        ],
        out_specs=[rows(SGU_WIDTH, 0), rows(RET_V_WIDTH, 0)],
        out_shape=[
            jax.ShapeDtypeStruct((m, SGU_WIDTH), BF16),
            jax.ShapeDtypeStruct((m, RET_V_WIDTH), BF16),
        ],
        scratch_shapes=[pltpu.VMEM((RET_HEADS, RET_DK, RET_DV), F32)],
        compiler_params=_cparams(2),
        name="mixer",
    )(uv, uv, qk, qk, rv, srg, sgu_g, sgu_w, sgu_bias, ret_g, decay, xi_map, zeta_map, cd)


def _resident(shape):
    return pl.BlockSpec(shape, lambda *_: (0,) * len(shape), pipeline_mode=pl.Buffered(1))


def _mix_out_kernel(a_ref, b_ref, sga_ref, sgb_ref, x_ref, wa_ref, wb_ref, wo_ref, g_ref, xo_ref, h_ref):
    ya = _dot(a_ref[...], wa_ref[...])
    yb = _dot(b_ref[...], wb_ref[...])
    m = (sga_ref[...].astype(F32) * ya + sgb_ref[...].astype(F32) * yb).astype(BF16)
    xn = x_ref[...] + _dot(m, wo_ref[...])
    xo_ref[...] = xn
    h_ref[...] = _rms_rows(xn, g_ref[...]).astype(h_ref.dtype)


def _mix_out(a, b, sg, x2, wa, wb, wo, g, tm=256):
    m, d = x2.shape
    return pl.pallas_call(
        _mix_out_kernel,
        grid=(m // tm,),
        in_specs=[
            pl.BlockSpec((tm, a.shape[1]), lambda i: (i, 0)),
            pl.BlockSpec((tm, b.shape[1]), lambda i: (i, 0)),
            pl.BlockSpec((tm, d), lambda i: (i, 0)),
            pl.BlockSpec((tm, d), lambda i: (i, 1)),
            pl.BlockSpec((tm, d), lambda i: (i, 0)),
            _resident(wa.shape), _resident(wb.shape), _resident(wo.shape), _resident(g.shape),
        ],
        out_specs=[pl.BlockSpec((tm, d), lambda i: (i, 0)), pl.BlockSpec((tm, d), lambda i: (i, 0))],
        out_shape=[jax.ShapeDtypeStruct((m, d), F32), jax.ShapeDtypeStruct((m, d), BF16)],
        compiler_params=_cparams(1),
        name="mix_out",
    )(a, b, sg, sg, x2, wa, wb, wo, g)


def _ffn_up_kernel(h_ref, wg_ref, wu_ref, cw_ref, cb_ref, o_ref, wgb_ref, wub_ref, tail_ref,
                   *, blocks_per_seq):
    i = pl.program_id(1)

    @pl.when(i == 0)
    def _():
        wgb_ref[...] = wg_ref[...].astype(BF16)
        wub_ref[...] = wu_ref[...].astype(BF16)

    @pl.when(i % blocks_per_seq == 0)
    def _():
        tail_ref[...] = jnp.zeros_like(tail_ref)

    h = h_ref[...]
    gate = _dot(h, wgb_ref[...])
    up = _dot(h, wub_ref[...])
    tm = gate.shape[0]
    ext = jnp.concatenate([tail_ref[...], gate], axis=0)
    tail_ref[...] = gate[tm - SUBLANES:, :]
    g_t1 = pltpu.roll(ext, 1, 0)[SUBLANES:, :]
    g_t2 = pltpu.roll(ext, 2, 0)[SUBLANES:, :]
    a = g_t2 * cw_ref[0:1, :]
    a = a + g_t1 * cw_ref[1:2, :]
    a = a + gate * cw_ref[2:3, :]
    a = a + cb_ref[...]
    o_ref[...] = (jax.nn.gelu(a) * up).astype(o_ref.dtype)


def _ffn_up(h, wg, wu, layer, cw, cb, seq, tm=1024, tf=512):
    m, d = h.shape
    f = wg.shape[2]
    kern = functools.partial(_ffn_up_kernel, blocks_per_seq=seq // tm)
    return pl.pallas_call(
        kern,
        grid=(f // tf, m // tm),
        in_specs=[
            pl.BlockSpec((tm, d), lambda j, i: (i, 0)),
            pl.BlockSpec((None, d, tf), lambda j, i: (layer, 0, j)),
            pl.BlockSpec((None, d, tf), lambda j, i: (layer, 0, j)),
            pl.BlockSpec((CONV_W, tf), lambda j, i: (0, j)),
            pl.BlockSpec((1, tf), lambda j, i: (0, j)),
        ],
        out_specs=pl.BlockSpec((tm, tf), lambda j, i: (i, j)),
        out_shape=jax.ShapeDtypeStruct((m, f), BF16),
        scratch_shapes=[pltpu.VMEM((d, tf), BF16), pltpu.VMEM((d, tf), BF16),
                        pltpu.VMEM((SUBLANES, tf), F32)],
        compiler_params=_cparams(2),
        name="ffn_up",
    )(h, wg, wu, cw, cb)


def _ffn_down_kernel(a_ref, x_ref, w_ref, o_ref):
    o_ref[...] = x_ref[...] + _dot(a_ref[...], w_ref[...])


def _ffn_down(act, x2, w, tm=256):
    m, d = x2.shape
    return pl.pallas_call(
        _ffn_down_kernel,
        grid=(m // tm,),
        in_specs=[
            pl.BlockSpec((tm, act.shape[1]), lambda i: (i, 0)),
            pl.BlockSpec((tm, d), lambda i: (i, 0)),
            _resident(w.shape),
        ],
        out_specs=pl.BlockSpec((tm, d), lambda i: (i, 0)),
        out_shape=jax.ShapeDtypeStruct((m, d), F32),
        compiler_params=_cparams(1),
        name="ffn_down",
    )(act, x2, w)


PLE_SUB_ROWS = 256


def _ple_kernel(x_ref, p_ref, g_ref, wg_ref, wp_ref, gn_ref, *out_refs):
    for r in range(x_ref.shape[0] // PLE_SUB_ROWS):
        rows = slice(r * PLE_SUB_ROWS, (r + 1) * PLE_SUB_ROWS)
        x = x_ref[rows, :]
        h = _rms_rows(x, g_ref[...]).astype(BF16)
        gate = jax.nn.sigmoid(_dot(h, wg_ref[...]))
        proj = _dot(p_ref[rows, :].astype(BF16), wp_ref[...])
        xn = x + proj * gate
        if len(out_refs) == 2:
            out_refs[0][rows, :] = xn
        out_refs[-1][rows, :] = _rms_rows(xn, gn_ref[...]).astype(out_refs[-1].dtype)


def _ple(x2, p2, g, wg, wp, g_next, last, tm=512):
    m, d = x2.shape
    row_spec = pl.BlockSpec((tm, d), lambda i: (i, 0))
    if last:
        out_specs, out_shape = [row_spec], [jax.ShapeDtypeStruct((m, d), F32)]
    else:
        out_specs = [row_spec, row_spec]
        out_shape = [jax.ShapeDtypeStruct((m, d), F32), jax.ShapeDtypeStruct((m, d), BF16)]
    return pl.pallas_call(
        _ple_kernel,
        grid=(m // tm,),
        in_specs=[
            row_spec,
            pl.BlockSpec((tm, PLE_DIM), lambda i: (i, 0)),
            _resident(g.shape), _resident(wg.shape), _resident(wp.shape), _resident(g_next.shape),
        ],
        out_specs=out_specs,
        out_shape=out_shape,
        compiler_params=_cparams(1),
        name="ple",
    )(x2, p2, g, wg, wp, g_next)


def _rotary_tables(seq):
    half = RET_DK // 2
    inv = ROPE_THETA ** (-jnp.arange(half, dtype=F32) / half)
    ang = jnp.arange(seq, dtype=jnp.int32).astype(F32)[:, None] * inv[None, :]
    cos = jnp.cos(ang)
    sin = jnp.sin(ang)
    return jnp.concatenate([cos, cos], axis=-1), jnp.concatenate([-sin, sin], axis=-1)


def _retention_tables():
    log_g = jnp.log1p(-jnp.exp2(-5.0 - jnp.arange(RET_HEADS, dtype=F32)))
    idx = jnp.arange(CHUNK, dtype=F32)
    diff = idx[:, None] - idx[None, :]
    causal = diff >= 0
    decay = jnp.where(causal[None], jnp.exp(log_g[:, None, None] * jnp.where(causal, diff, 0.0)[None]), 0.0)
    xi = jnp.exp(log_g[:, None] * (idx[None, :] + 1.0)).T
    zeta = jnp.exp(log_g[:, None] * (CHUNK - 1.0 - idx[None, :])).T
    chunk_decay = jnp.exp(log_g * CHUNK)
    xi_map = jnp.repeat(xi, RET_DV, axis=1)
    zeta_map = jnp.repeat(zeta, RET_DK, axis=1)
    cd = jnp.broadcast_to(chunk_decay[:, None, None], (RET_HEADS, 1, RET_DV))
    return decay, xi_map, zeta_map, cd


def kernel(x, p, mix_norm_g, w_in, sgu_norm_g, sgu_w, sgu_b, ret_norm_g, w_branch_a, w_branch_b, w_out, ffn_norm_g, ffn_w_gate, ffn_w_up, ffn_conv_w, ffn_conv_b, ffn_w_down, ple_norm_g, ple_w_gate, ple_w_proj, final_norm_g):
    batch, seq, d = x.shape
    depth = w_in.shape[0]
    m = batch * seq
    assert d == D_MODEL and seq % 1024 == 0 and w_in.shape[2] == IN_COLS

    cos_t, sin_t = _rotary_tables(seq)
    decay, xi_map, zeta_map, cd = _retention_tables()
    qk_scale = jnp.concatenate([jnp.ones((1, RET_QK_WIDTH), F32),
                                jnp.full((1, RET_QK_WIDTH), RET_DK ** -0.5, F32)], axis=1)
    rot = (cos_t, sin_t, qk_scale, seq)

    x2 = x.reshape(m, d)
    h = _norm_bf16(x2, mix_norm_g[0][None])
    for i in range(depth):
        uv, wa = _in_proj(h, w_in, i, COL_UV, 2 * SGU_WIDTH, jax.nn.gelu, sides=(w_branch_a,))
        qk, wb = _in_proj(h, w_in, i, COL_QK, 2 * RET_QK_WIDTH, None, sides=(w_branch_b,), rot=rot)
        rv, wo = _in_proj(h, w_in, i, COL_RV, RET_V_WIDTH, lambda z: z, sides=(w_out,))
        srg, wpg = _in_proj(h, w_in, i, COL_RG, RET_V_WIDTH, jax.nn.silu, sides=(ple_w_gate,))
        sg, wd = _in_proj(h, w_in, i, COL_G, 2 * D_MODEL, jax.nn.sigmoid, sides=(ffn_w_down,))
        sgu_bias = jnp.repeat(sgu_b[i].T, SGU_GROUP_DIM, axis=1)
        a, b = _mixer(uv, qk, rv, srg, sgu_norm_g[i][None], sgu_w[i], sgu_bias, ret_norm_g[i][None],
                      decay, xi_map, zeta_map, cd, batch, seq)
        x2, hf = _mix_out(a, b, sg, x2, wa, wb, wo, ffn_norm_g[i][None])
        act = _ffn_up(hf, ffn_w_gate, ffn_w_up, i, ffn_conv_w[i], ffn_conv_b[i][None], seq)
        x2 = _ffn_down(act, x2, wd)
        last = i == depth - 1
        g_next = final_norm_g[None] if last else mix_norm_g[i + 1][None]
        outs = _ple(x2, p[i].reshape(m, PLE_DIM), ple_norm_g[i][None], wpg, ple_w_proj[i].astype(BF16),
                    g_next, last)
        if last:
            (out,) = outs
        else:
            x2, h = outs
    return out.reshape(batch, seq, d)
```

```python
import functools

import jax
import jax.numpy as jnp
from jax import lax
from jax.experimental import pallas as pl
from jax.experimental.pallas import tpu as pltpu

D_MODEL = 2048
CHUNK = 128
SGU_GROUPS = 8
SGU_GROUP_DIM = 128
SGU_WIDTH = SGU_GROUPS * SGU_GROUP_DIM
RET_HEADS = 8
RET_DK = 128
RET_DV = 256
RET_QK_WIDTH = RET_HEADS * RET_DK
RET_V_WIDTH = RET_HEADS * RET_DV
ROPE_THETA = 10000.0
D_FF = 5632
CONV_W = 3
PLE_DIM = 256
NORM_EPS = 1e-6
IN_COLS = 2 * SGU_WIDTH + 2 * RET_QK_WIDTH + 2 * RET_V_WIDTH + 2 * D_MODEL

V7X_VMEM_LIMIT_BYTES = 56000 * 1024
SUBLANES = 8
BF16_ROWS = 16

BF16 = jnp.bfloat16
F32 = jnp.float32

COL_UV = 0
COL_QK = 2 * SGU_WIDTH
COL_RV = COL_QK + 2 * RET_QK_WIDTH
COL_RG = COL_RV + RET_V_WIDTH
COL_G = COL_RG + RET_V_WIDTH


def _cparams(n_axes):
    return pltpu.CompilerParams(
        dimension_semantics=("arbitrary",) * n_axes,
        vmem_limit_bytes=V7X_VMEM_LIMIT_BYTES,
    )


def _rms_rows(x, g):
    y = x * lax.rsqrt(jnp.mean(x * x, axis=-1, keepdims=True) + NORM_EPS)
    return y * g


def _dot(a, b):
    return jnp.dot(a, b, preferred_element_type=F32)


def _norm_kernel(x_ref, g_ref, o_ref):
    o_ref[...] = _rms_rows(x_ref[...], g_ref[...]).astype(o_ref.dtype)


def _norm_bf16(x2, g, tm=512):
    m, d = x2.shape
    return pl.pallas_call(
        _norm_kernel,
        grid=(m // tm,),
        in_specs=[pl.BlockSpec((tm, d), lambda i: (i, 0)), pl.BlockSpec((1, d), lambda i: (0, 0))],
        out_specs=pl.BlockSpec((tm, d), lambda i: (i, 0)),
        out_shape=jax.ShapeDtypeStruct((m, d), BF16),
        compiler_params=_cparams(1),
        name="norm",
    )(x2, g)


PROJ_SUB_ROWS = 128


def _cast_weight_once(w_ref, wb_ref):
    @pl.when(pl.program_id(1) == 0)
    def _():
        wb_ref[...] = w_ref[...].astype(BF16)


def _cast_sides(side_refs):
    n = len(side_refs) // 2
    for src, dst in zip(side_refs[:n], side_refs[n:]):
        dst[...] = src[...].astype(dst.dtype)


def _proj_kernel(h_ref, w_ref, *rest, act, n_side):
    side_in, (o_ref,), side_out, (wb_ref,) = (rest[:n_side], rest[n_side:n_side + 1],
                                              rest[n_side + 1:2 * n_side + 1], rest[2 * n_side + 1:])
    _cast_weight_once(w_ref, wb_ref)
    _cast_sides(side_in + side_out)
    for r in range(h_ref.shape[0] // PROJ_SUB_ROWS):
        rows = slice(r * PROJ_SUB_ROWS, (r + 1) * PROJ_SUB_ROWS)
        o_ref[rows, :] = act(_dot(h_ref[rows, :], wb_ref[...])).astype(o_ref.dtype)


def _rot_kernel(h_ref, w_ref, cos_ref, sin_ref, scale_ref, *rest, n_side):
    side_in, (o_ref,), side_out, (wb_ref,) = (rest[:n_side], rest[n_side:n_side + 1],
                                              rest[n_side + 1:2 * n_side + 1], rest[2 * n_side + 1:])
    _cast_weight_once(w_ref, wb_ref)
    _cast_sides(side_in + side_out)
    for r in range(h_ref.shape[0] // PROJ_SUB_ROWS):
        rows = slice(r * PROJ_SUB_ROWS, (r + 1) * PROJ_SUB_ROWS)
        z = _dot(h_ref[rows, :], wb_ref[...])
        cos = cos_ref[rows, :]
        sin = sin_ref[rows, :]
        for h in range(z.shape[1] // RET_DK):
            sl = slice(h * RET_DK, (h + 1) * RET_DK)
            zh = z[:, sl]
            rot = zh * cos + pltpu.roll(zh, RET_DK // 2, 1) * sin
            o_ref[rows, sl] = (rot * scale_ref[:, sl]).astype(o_ref.dtype)


def _side_specs(sides, layer, n_steps, inner):
    in_specs, out_specs, out_shapes = [], [], []
    for w in sides:
        _, rows, cols = w.shape
        slab = rows // n_steps
        assert slab * n_steps == rows and slab % BF16_ROWS == 0
        in_specs.append(pl.BlockSpec((None, slab, cols), lambda j, i: (layer, j * inner + i, 0)))
        out_specs.append(pl.BlockSpec((slab, cols), lambda j, i: (j * inner + i, 0)))
        out_shapes.append(jax.ShapeDtypeStruct((rows, cols), BF16))
    return in_specs, out_specs, out_shapes


def _in_proj(h, w, layer, col0, n, act, sides=(), rot=None, tm=1024, tn=1024):
    m, d = h.shape
    off = col0 // tn
    grid = (n // tn, m // tm)
    s_in, s_out, s_shapes = _side_specs(sides, layer, grid[0] * grid[1], grid[1])
    in_specs = [
        pl.BlockSpec((tm, d), lambda j, i: (i, 0)),
        pl.BlockSpec((None, d, tn), lambda j, i: (layer, 0, off + j)),
    ]
    args = [h, w]
    if rot is None:
        kern = functools.partial(_proj_kernel, act=act, n_side=len(sides))
    else:
        cos_t, sin_t, scale, seq = rot
        pos_blocks = seq // tm
        in_specs += [
            pl.BlockSpec((tm, RET_DK), lambda j, i: (i % pos_blocks, 0)),
            pl.BlockSpec((tm, RET_DK), lambda j, i: (i % pos_blocks, 0)),
            pl.BlockSpec((1, tn), lambda j, i: (0, j)),
        ]
        args += [cos_t, sin_t, scale]
        kern = functools.partial(_rot_kernel, n_side=len(sides))
    return pl.pallas_call(
        kern,
        grid=grid,
        in_specs=in_specs + s_in,
        out_specs=[pl.BlockSpec((tm, tn), lambda j, i: (i, j))] + s_out,
        out_shape=[jax.ShapeDtypeStruct((m, n), BF16)] + s_shapes,
        scratch_shapes=[pltpu.VMEM((d, tn), BF16)],
        compiler_params=_cparams(2),
        name="in_proj_rot" if rot is not None else "in_proj",
    )(*args, *sides)


MIXER_CHUNKS_PER_STEP = 4


def _mixer_kernel(gu_ref, gv_ref, q_ref, k_ref, rv_ref, srg_ref,
                  sgu_g_ref, sgu_w_ref, sgu_bias_ref, ret_g_ref,
                  decay_ref, xi_ref, zeta_ref, cd_ref,
                  a_ref, b_ref, state_ref):
    @pl.when(pl.program_id(1) == 0)
    def _():
        state_ref[...] = jnp.zeros_like(state_ref)

    row = lax.broadcasted_iota(jnp.int32, (CHUNK, CHUNK), 0)
    col = lax.broadcasted_iota(jnp.int32, (CHUNK, CHUNK), 1)
    causal = row >= col
    for c in range(MIXER_CHUNKS_PER_STEP):
        rows = slice(c * CHUNK, (c + 1) * CHUNK)

        vn = _rms_rows(gv_ref[rows, :].astype(F32), sgu_g_ref[...]).astype(BF16)
        for g in range(SGU_GROUPS):
            sl = slice(g * SGU_GROUP_DIM, (g + 1) * SGU_GROUP_DIM)
            w = jnp.where(causal, sgu_w_ref[g], 0.0).astype(BF16)
            s = _dot(w, vn[:, sl]) + sgu_bias_ref[:, sl]
            a_ref[rows, sl] = (gu_ref[rows, sl].astype(F32) * s).astype(a_ref.dtype)

        for h in range(RET_HEADS):
            ksl = slice(h * RET_DK, (h + 1) * RET_DK)
            vsl = slice(h * RET_DV, (h + 1) * RET_DV)
            q = q_ref[rows, ksl]
            k = k_ref[rows, ksl]
            v = rv_ref[rows, vsl]
            scores = lax.dot_general(q, k, (((1,), (1,)), ((), ())), preferred_element_type=F32)
            scores = (scores * decay_ref[h]).astype(BF16)
            inner = _dot(scores, v)
            prev = state_ref[h]
            cross = _dot(q, prev.astype(BF16)) * xi_ref[:, vsl]
            kz = (k.astype(F32) * zeta_ref[:, ksl]).astype(BF16)
            kv = lax.dot_general(kz, v, (((0,), (0,)), ((), ())), preferred_element_type=F32)
            state_ref[h] = prev * cd_ref[h] + kv
            y = _rms_rows(inner + cross, ret_g_ref[:, vsl])
            b_ref[rows, vsl] = (srg_ref[rows, vsl].astype(F32) * y).astype(b_ref.dtype)


def _mixer(uv, qk, rv, srg, sgu_g, sgu_w, sgu_bias, ret_g, decay, xi_map, zeta_map, cd, batch, seq):
    m = uv.shape[0]
    tm = MIXER_CHUNKS_PER_STEP * CHUNK
    steps = seq // tm

    def rows(width, blk):
        return pl.BlockSpec((tm, width), lambda b, n: (b * steps + n, blk))

    def const(shape):
        return pl.BlockSpec(shape, lambda b, n: (0,) * len(shape))

    return pl.pallas_call(
        _mixer_kernel,
        grid=(batch, steps),
        in_specs=[
            rows(SGU_WIDTH, 0), rows(SGU_WIDTH, 1),
            rows(RET_QK_WIDTH, 0), rows(RET_QK_WIDTH, 1),
            rows(RET_V_WIDTH, 0), rows(RET_V_WIDTH, 0),
            const((1, SGU_WIDTH)), const((SGU_GROUPS, CHUNK, CHUNK)), const((CHUNK, SGU_WIDTH)),
            const((1, RET_V_WIDTH)),
            const((RET_HEADS, CHUNK, CHUNK)), const((CHUNK, RET_V_WIDTH)), const((CHUNK, RET_QK_WIDTH)),
            const((RET_HEADS, 1, RET_DV)),
        ],
        out_specs=[rows(SGU_WIDTH, 0), rows(RET_V_WIDTH, 0)],
        out_shape=[
            jax.ShapeDtypeStruct((m, SGU_WIDTH), BF16),
            jax.ShapeDtypeStruct((m, RET_V_WIDTH), BF16),
        ],
        scratch_shapes=[pltpu.VMEM((RET_HEADS, RET_DK, RET_DV), F32)],
        compiler_params=_cparams(2),
        name="mixer",
    )(uv, uv, qk, qk, rv, srg, sgu_g, sgu_w, sgu_bias, ret_g, decay, xi_map, zeta_map, cd)


def _resident(shape):
    return pl.BlockSpec(shape, lambda *_: (0,) * len(shape), pipeline_mode=pl.Buffered(1))


def _mix_out_kernel(a_ref, b_ref, sga_ref, sgb_ref, x_ref, wa_ref, wb_ref, wo_ref, g_ref, xo_ref, h_ref):
    ya = _dot(a_ref[...], wa_ref[...])
    yb = _dot(b_ref[...], wb_ref[...])
    m = (sga_ref[...].astype(F32) * ya + sgb_ref[...].astype(F32) * yb).astype(BF16)
    xn = x_ref[...] + _dot(m, wo_ref[...])
    xo_ref[...] = xn
    h_ref[...] = _rms_rows(xn, g_ref[...]).astype(h_ref.dtype)


def _mix_out(a, b, sg, x2, wa, wb, wo, g, tm=256):
    m, d = x2.shape
    return pl.pallas_call(
        _mix_out_kernel,
        grid=(m // tm,),
        in_specs=[
            pl.BlockSpec((tm, a.shape[1]), lambda i: (i, 0)),
            pl.BlockSpec((tm, b.shape[1]), lambda i: (i, 0)),
            pl.BlockSpec((tm, d), lambda i: (i, 0)),
            pl.BlockSpec((tm, d), lambda i: (i, 1)),
            pl.BlockSpec((tm, d), lambda i: (i, 0)),
            _resident(wa.shape), _resident(wb.shape), _resident(wo.shape), _resident(g.shape),
        ],
        out_specs=[pl.BlockSpec((tm, d), lambda i: (i, 0)), pl.BlockSpec((tm, d), lambda i: (i, 0))],
        out_shape=[jax.ShapeDtypeStruct((m, d), F32), jax.ShapeDtypeStruct((m, d), BF16)],
        compiler_params=_cparams(1),
        name="mix_out",
    )(a, b, sg, sg, x2, wa, wb, wo, g)


def _ffn_up_kernel(h_ref, wg_ref, wu_ref, cw_ref, cb_ref, o_ref, wgb_ref, wub_ref, tail_ref,
                   *, blocks_per_seq):
    i = pl.program_id(1)

    @pl.when(i == 0)
    def _():
        wgb_ref[...] = wg_ref[...].astype(BF16)
        wub_ref[...] = wu_ref[...].astype(BF16)

    @pl.when(i % blocks_per_seq == 0)
    def _():
        tail_ref[...] = jnp.zeros_like(tail_ref)

    h = h_ref[...]
    gate = _dot(h, wgb_ref[...])
    up = _dot(h, wub_ref[...])
    tm = gate.shape[0]
    ext = jnp.concatenate([tail_ref[...], gate], axis=0)
    tail_ref[...] = gate[tm - SUBLANES:, :]
    g_t1 = pltpu.roll(ext, 1, 0)[SUBLANES:, :]
    g_t2 = pltpu.roll(ext, 2, 0)[SUBLANES:, :]
    a = g_t2 * cw_ref[0:1, :]
    a = a + g_t1 * cw_ref[1:2, :]
    a = a + gate * cw_ref[2:3, :]
    a = a + cb_ref[...]
    o_ref[...] = (jax.nn.gelu(a) * up).astype(o_ref.dtype)


def _ffn_up(h, wg, wu, layer, cw, cb, seq, tm=1024, tf=512):
    m, d = h.shape
    f = wg.shape[2]
    kern = functools.partial(_ffn_up_kernel, blocks_per_seq=seq // tm)
    return pl.pallas_call(
        kern,
        grid=(f // tf, m // tm),
        in_specs=[
            pl.BlockSpec((tm, d), lambda j, i: (i, 0)),
            pl.BlockSpec((None, d, tf), lambda j, i: (layer, 0, j)),
            pl.BlockSpec((None, d, tf), lambda j, i: (layer, 0, j)),
            pl.BlockSpec((CONV_W, tf), lambda j, i: (0, j)),
            pl.BlockSpec((1, tf), lambda j, i: (0, j)),
        ],
        out_specs=pl.BlockSpec((tm, tf), lambda j, i: (i, j)),
        out_shape=jax.ShapeDtypeStruct((m, f), BF16),
        scratch_shapes=[pltpu.VMEM((d, tf), BF16), pltpu.VMEM((d, tf), BF16),
                        pltpu.VMEM((SUBLANES, tf), F32)],
        compiler_params=_cparams(2),
        name="ffn_up",
    )(h, wg, wu, cw, cb)


def _ffn_down_kernel(a_ref, x_ref, w_ref, o_ref):
    o_ref[...] = x_ref[...] + _dot(a_ref[...], w_ref[...])


def _ffn_down(act, x2, w, tm=256):
    m, d = x2.shape
    return pl.pallas_call(
        _ffn_down_kernel,
        grid=(m // tm,),
        in_specs=[
            pl.BlockSpec((tm, act.shape[1]), lambda i: (i, 0)),
            pl.BlockSpec((tm, d), lambda i: (i, 0)),
            _resident(w.shape),
        ],
        out_specs=pl.BlockSpec((tm, d), lambda i: (i, 0)),
        out_shape=jax.ShapeDtypeStruct((m, d), F32),
        compiler_params=_cparams(1),
        name="ffn_down",
    )(act, x2, w)


PLE_SUB_ROWS = 256


def _ple_kernel(x_ref, p_ref, g_ref, wg_ref, wp_ref, gn_ref, *out_refs):
    for r in range(x_ref.shape[0] // PLE_SUB_ROWS):
        rows = slice(r * PLE_SUB_ROWS, (r + 1) * PLE_SUB_ROWS)
        x = x_ref[rows, :]
        h = _rms_rows(x, g_ref[...]).astype(BF16)
        gate = jax.nn.sigmoid(_dot(h, wg_ref[...]))
        proj = _dot(p_ref[rows, :].astype(BF16), wp_ref[...])
        xn = x + proj * gate
        if len(out_refs) == 2:
            out_refs[0][rows, :] = xn
        out_refs[-1][rows, :] = _rms_rows(xn, gn_ref[...]).astype(out_refs[-1].dtype)


def _ple(x2, p2, g, wg, wp, g_next, last, tm=512):
    m, d = x2.shape
    row_spec = pl.BlockSpec((tm, d), lambda i: (i, 0))
    if last:
        out_specs, out_shape = [row_spec], [jax.ShapeDtypeStruct((m, d), F32)]
    else:
        out_specs = [row_spec, row_spec]
        out_shape = [jax.ShapeDtypeStruct((m, d), F32), jax.ShapeDtypeStruct((m, d), BF16)]
    return pl.pallas_call(
        _ple_kernel,
        grid=(m // tm,),
        in_specs=[
            row_spec,
            pl.BlockSpec((tm, PLE_DIM), lambda i: (i, 0)),
            _resident(g.shape), _resident(wg.shape), _resident(wp.shape), _resident(g_next.shape),
        ],
        out_specs=out_specs,
        out_shape=out_shape,
        compiler_params=_cparams(1),
        name="ple",
    )(x2, p2, g, wg, wp, g_next)


def _rotary_tables(seq):
    half = RET_DK // 2
    inv = ROPE_THETA ** (-jnp.arange(half, dtype=F32) / half)
    ang = jnp.arange(seq, dtype=jnp.int32).astype(F32)[:, None] * inv[None, :]
    cos = jnp.cos(ang)
    sin = jnp.sin(ang)
    return jnp.concatenate([cos, cos], axis=-1), jnp.concatenate([-sin, sin], axis=-1)


def _retention_tables():
    log_g = jnp.log1p(-jnp.exp2(-5.0 - jnp.arange(RET_HEADS, dtype=F32)))
    idx = jnp.arange(CHUNK, dtype=F32)
    diff = idx[:, None] - idx[None, :]
    causal = diff >= 0
    decay = jnp.where(causal[None], jnp.exp(log_g[:, None, None] * jnp.where(causal, diff, 0.0)[None]), 0.0)
    xi = jnp.exp(log_g[:, None] * (idx[None, :] + 1.0)).T
    zeta = jnp.exp(log_g[:, None] * (CHUNK - 1.0 - idx[None, :])).T
    chunk_decay = jnp.exp(log_g * CHUNK)
    xi_map = jnp.repeat(xi, RET_DV, axis=1)
    zeta_map = jnp.repeat(zeta, RET_DK, axis=1)
    cd = jnp.broadcast_to(chunk_decay[:, None, None], (RET_HEADS, 1, RET_DV))
    return decay, xi_map, zeta_map, cd


def kernel(x, p, mix_norm_g, w_in, sgu_norm_g, sgu_w, sgu_b, ret_norm_g, w_branch_a, w_branch_b, w_out, ffn_norm_g, ffn_w_gate, ffn_w_up, ffn_conv_w, ffn_conv_b, ffn_w_down, ple_norm_g, ple_w_gate, ple_w_proj, final_norm_g):
    batch, seq, d = x.shape
    depth = w_in.shape[0]
    m = batch * seq
    assert d == D_MODEL and seq % 1024 == 0 and w_in.shape[2] == IN_COLS

    cos_t, sin_t = _rotary_tables(seq)
    decay, xi_map, zeta_map, cd = _retention_tables()
    qk_scale = jnp.concatenate([jnp.ones((1, RET_QK_WIDTH), F32),
                                jnp.full((1, RET_QK_WIDTH), RET_DK ** -0.5, F32)], axis=1)
    rot = (cos_t, sin_t, qk_scale, seq)

    x2 = x.reshape(m, d)
    h = _norm_bf16(x2, mix_norm_g[0][None])
    for i in range(depth):
        uv, wa = _in_proj(h, w_in, i, COL_UV, 2 * SGU_WIDTH, jax.nn.gelu, sides=(w_branch_a,))
        qk, wb = _in_proj(h, w_in, i, COL_QK, 2 * RET_QK_WIDTH, None, sides=(w_branch_b,), rot=rot)
        rv, wo = _in_proj(h, w_in, i, COL_RV, RET_V_WIDTH, lambda z: z, sides=(w_out,))
        srg, wpg = _in_proj(h, w_in, i, COL_RG, RET_V_WIDTH, jax.nn.silu, sides=(ple_w_gate,))
        sg, wd = _in_proj(h, w_in, i, COL_G, 2 * D_MODEL, jax.nn.sigmoid, sides=(ffn_w_down,))
        sgu_bias = jnp.repeat(sgu_b[i].T, SGU_GROUP_DIM, axis=1)
        a, b = _mixer(uv, qk, rv, srg, sgu_norm_g[i][None], sgu_w[i], sgu_bias, ret_norm_g[i][None],
                      decay, xi_map, zeta_map, cd, batch, seq)
        x2, hf = _mix_out(a, b, sg, x2, wa, wb, wo, ffn_norm_g[i][None])
        act = _ffn_up(hf, ffn_w_gate, ffn_w_up, i, ffn_conv_w[i], ffn_conv_b[i][None], seq)
        x2 = _ffn_down(act, x2, wd)
        last = i == depth - 1
        g_next = final_norm_g[None] if last else mix_norm_g[i + 1][None]
        outs = _ple(x2, p[i].reshape(m, PLE_DIM), ple_norm_g[i][None], wpg, ple_w_proj[i].astype(BF16),
                    g_next, last)
        if last:
            (out,) = outs
        else:
            x2, h = outs
    return out.reshape(batch, seq, d)
```

```python
import functools

import jax
import jax.numpy as jnp
from jax import lax
from jax.experimental import pallas as pl
from jax.experimental.pallas import tpu as pltpu

D_MODEL = 2048
CHUNK = 128
SGU_GROUPS = 8
SGU_GROUP_DIM = 128
SGU_WIDTH = SGU_GROUPS * SGU_GROUP_DIM
RET_HEADS = 8
RET_DK = 128
RET_DV = 256
RET_QK_WIDTH = RET_HEADS * RET_DK
RET_V_WIDTH = RET_HEADS * RET_DV
ROPE_THETA = 10000.0
D_FF = 5632
CONV_W = 3
PLE_DIM = 256
NORM_EPS = 1e-6
IN_COLS = 2 * SGU_WIDTH + 2 * RET_QK_WIDTH + 2 * RET_V_WIDTH + 2 * D_MODEL

V7X_VMEM_LIMIT_BYTES = 62 * 1024 * 1024
SUBLANES = 8
BF16_ROWS = 16

BF16 = jnp.bfloat16
F32 = jnp.float32

COL_UV = 0
COL_QK = 2 * SGU_WIDTH
COL_RV = COL_QK + 2 * RET_QK_WIDTH
COL_RG = COL_RV + RET_V_WIDTH
COL_G = COL_RG + RET_V_WIDTH


def _cparams(n_axes):
    return pltpu.CompilerParams(
        dimension_semantics=("arbitrary",) * n_axes,
        vmem_limit_bytes=V7X_VMEM_LIMIT_BYTES,
    )


def _rms_rows(x, g):
    y = x * lax.rsqrt(jnp.mean(x * x, axis=-1, keepdims=True) + NORM_EPS)
    return y * g


def _dot(a, b):
    return jnp.dot(a, b, preferred_element_type=F32)


def _norm_kernel(x_ref, g_ref, o_ref):
    o_ref[...] = _rms_rows(x_ref[...], g_ref[...]).astype(o_ref.dtype)


def _norm_bf16(x2, g, tm=512):
    m, d = x2.shape
    return pl.pallas_call(
        _norm_kernel,
        grid=(m // tm,),
        in_specs=[pl.BlockSpec((tm, d), lambda i: (i, 0)), pl.BlockSpec((1, d), lambda i: (0, 0))],
        out_specs=pl.BlockSpec((tm, d), lambda i: (i, 0)),
        out_shape=jax.ShapeDtypeStruct((m, d), BF16),
        compiler_params=_cparams(1),
        name="norm",
    )(x2, g)


PROJ_SUB_ROWS = 256


def _cast_weight_once(w_ref, wb_ref):
    @pl.when(pl.program_id(1) == 0)
    def _():
        wb_ref[...] = w_ref[...].astype(BF16)


def _cast_sides(side_refs):
    n = len(side_refs) // 2
    for src, dst in zip(side_refs[:n], side_refs[n:]):
        dst[...] = src[...].astype(dst.dtype)


def _proj_kernel(h_ref, w_ref, *rest, act, n_side, sub):
    side_in, (o_ref,), side_out, (wb_ref,) = (rest[:n_side], rest[n_side:n_side + 1],
                                              rest[n_side + 1:2 * n_side + 1], rest[2 * n_side + 1:])
    _cast_weight_once(w_ref, wb_ref)
    _cast_sides(side_in + side_out)
    for r in range(h_ref.shape[0] // sub):
        rows = slice(r * sub, (r + 1) * sub)
        o_ref[rows, :] = act(_dot(h_ref[rows, :], wb_ref[...])).astype(o_ref.dtype)


def _rot_kernel(h_ref, w_ref, cos_ref, sin_ref, scale_ref, *rest, n_side, sub):
    side_in, (o_ref,), side_out, (wb_ref,) = (rest[:n_side], rest[n_side:n_side + 1],
                                              rest[n_side + 1:2 * n_side + 1], rest[2 * n_side + 1:])
    _cast_weight_once(w_ref, wb_ref)
    _cast_sides(side_in + side_out)
    for r in range(h_ref.shape[0] // sub):
        rows = slice(r * sub, (r + 1) * sub)
        z = _dot(h_ref[rows, :], wb_ref[...])
        cos = cos_ref[rows, :]
        sin = sin_ref[rows, :]
        for h in range(z.shape[1] // RET_DK):
            sl = slice(h * RET_DK, (h + 1) * RET_DK)
            zh = z[:, sl]
            rot = zh * cos + pltpu.roll(zh, RET_DK // 2, 1) * sin
            o_ref[rows, sl] = (rot * scale_ref[:, sl]).astype(o_ref.dtype)


def _side_specs(sides, layer, n_steps, inner):
    in_specs, out_specs, out_shapes = [], [], []
    for w in sides:
        _, rows, cols = w.shape
        slab = rows // n_steps
        assert slab * n_steps == rows and slab % BF16_ROWS == 0
        in_specs.append(pl.BlockSpec((None, slab, cols), lambda j, i: (layer, j * inner + i, 0)))
        out_specs.append(pl.BlockSpec((slab, cols), lambda j, i: (j * inner + i, 0)))
        out_shapes.append(jax.ShapeDtypeStruct((rows, cols), BF16))
    return in_specs, out_specs, out_shapes


def _in_proj(h, w, layer, col0, n, act, sides=(), rot=None, tm=2048, tn=1024, sub=PROJ_SUB_ROWS):
    m, d = h.shape
    off = col0 // tn
    grid = (n // tn, m // tm)
    s_in, s_out, s_shapes = _side_specs(sides, layer, grid[0] * grid[1], grid[1])
    in_specs = [
        pl.BlockSpec((tm, d), lambda j, i: (i, 0)),
        pl.BlockSpec((None, d, tn), lambda j, i: (layer, 0, off + j)),
    ]
    args = [h, w]
    if rot is None:
        kern = functools.partial(_proj_kernel, act=act, n_side=len(sides), sub=sub)
    else:
        cos_t, sin_t, scale, seq = rot
        pos_blocks = seq // tm
        in_specs += [
            pl.BlockSpec((tm, RET_DK), lambda j, i: (i % pos_blocks, 0)),
            pl.BlockSpec((tm, RET_DK), lambda j, i: (i % pos_blocks, 0)),
            pl.BlockSpec((1, tn), lambda j, i: (0, j)),
        ]
        args += [cos_t, sin_t, scale]
        kern = functools.partial(_rot_kernel, n_side=len(sides), sub=sub)
    return pl.pallas_call(
        kern,
        grid=grid,
        in_specs=in_specs + s_in,
        out_specs=[pl.BlockSpec((tm, tn), lambda j, i: (i, j))] + s_out,
        out_shape=[jax.ShapeDtypeStruct((m, n), BF16)] + s_shapes,
        scratch_shapes=[pltpu.VMEM((d, tn), BF16)],
        compiler_params=_cparams(2),
        name="in_proj_rot" if rot is not None else "in_proj",
    )(*args, *sides)


MIXER_CHUNKS_PER_STEP = 4


def _mixer_kernel(gu_ref, gv_ref, q_ref, k_ref, rv_ref, srg_ref,
                  sgu_g_ref, sgu_w_ref, sgu_bias_ref, ret_g_ref,
                  decay_ref, xi_ref, zeta_ref, cd_ref,
                  a_ref, b_ref, state_ref):
    @pl.when(pl.program_id(1) == 0)
    def _():
        state_ref[...] = jnp.zeros_like(state_ref)

    row = lax.broadcasted_iota(jnp.int32, (CHUNK, CHUNK), 0)
    col = lax.broadcasted_iota(jnp.int32, (CHUNK, CHUNK), 1)
    causal = row >= col
    for c in range(MIXER_CHUNKS_PER_STEP):
        rows = slice(c * CHUNK, (c + 1) * CHUNK)

        vn = _rms_rows(gv_ref[rows, :].astype(F32), sgu_g_ref[...]).astype(BF16)
        for g in range(SGU_GROUPS):
            sl = slice(g * SGU_GROUP_DIM, (g + 1) * SGU_GROUP_DIM)
            w = jnp.where(causal, sgu_w_ref[g], 0.0).astype(BF16)
            s = _dot(w, vn[:, sl]) + sgu_bias_ref[:, sl]
            a_ref[rows, sl] = (gu_ref[rows, sl].astype(F32) * s).astype(a_ref.dtype)

        for h in range(RET_HEADS):
            ksl = slice(h * RET_DK, (h + 1) * RET_DK)
            vsl = slice(h * RET_DV, (h + 1) * RET_DV)
            q = q_ref[rows, ksl]
            k = k_ref[rows, ksl]
            v = rv_ref[rows, vsl]
            scores = lax.dot_general(q, k, (((1,), (1,)), ((), ())), preferred_element_type=F32)
            scores = (scores * decay_ref[h]).astype(BF16)
            inner = _dot(scores, v)
            prev = state_ref[h]
            cross = _dot(q, prev.astype(BF16)) * xi_ref[:, vsl]
            kz = (k.astype(F32) * zeta_ref[:, ksl]).astype(BF16)
            kv = lax.dot_general(kz, v, (((0,), (0,)), ((), ())), preferred_element_type=F32)
            state_ref[h] = prev * cd_ref[h] + kv
            y = _rms_rows(inner + cross, ret_g_ref[:, vsl])
            b_ref[rows, vsl] = (srg_ref[rows, vsl].astype(F32) * y).astype(b_ref.dtype)


def _mixer(uv, qk, rv, srg, sgu_g, sgu_w, sgu_bias, ret_g, decay, xi_map, zeta_map, cd, batch, seq):
    m = uv.shape[0]
    tm = MIXER_CHUNKS_PER_STEP * CHUNK
    steps = seq // tm

    def rows(width, blk):
        return pl.BlockSpec((tm, width), lambda b, n: (b * steps + n, blk))

    def const(shape):
        return pl.BlockSpec(shape, lambda b, n: (0,) * len(shape))

    return pl.pallas_call(
        _mixer_kernel,
        grid=(batch, steps),
        in_specs=[
            rows(SGU_WIDTH, 0), rows(SGU_WIDTH, 1),
            rows(RET_QK_WIDTH, 0), rows(RET_QK_WIDTH, 1),
            rows(RET_V_WIDTH, 0), rows(RET_V_WIDTH, 0),
            const((1, SGU_WIDTH)), const((SGU_GROUPS, CHUNK, CHUNK)), const((CHUNK, SGU_WIDTH)),
            const((1, RET_V_WIDTH)),
            const((RET_HEADS, CHUNK, CHUNK)), const((CHUNK, RET_V_WIDTH)), const((CHUNK, RET_QK_WIDTH)),
            const((RET_HEADS, 1, RET_DV)),
        ],
        out_specs=[rows(SGU_WIDTH, 0), rows(RET_V_WIDTH, 0)],
        out_shape=[
            jax.ShapeDtypeStruct((m, SGU_WIDTH), BF16),
            jax.ShapeDtypeStruct((m, RET_V_WIDTH), BF16),
        ],
        scratch_shapes=[pltpu.VMEM((RET_HEADS, RET_DK, RET_DV), F32)],
        compiler_params=_cparams(2),
        name="mixer",
    )(uv, uv, qk, qk, rv, srg, sgu_g, sgu_w, sgu_bias, ret_g, decay, xi_map, zeta_map, cd)


def _resident(shape):
    return pl.BlockSpec(shape, lambda *_: (0,) * len(shape), pipeline_mode=pl.Buffered(1))


def _mix_out_kernel(a_ref, b_ref, sga_ref, sgb_ref, x_ref, wa_ref, wb_ref, wo_ref, g_ref, xo_ref, h_ref):
    ya = _dot(a_ref[...], wa_ref[...])
    yb = _dot(b_ref[...], wb_ref[...])
    m = (sga_ref[...].astype(F32) * ya + sgb_ref[...].astype(F32) * yb).astype(BF16)
    xn = x_ref[...] + _dot(m, wo_ref[...])
    xo_ref[...] = xn
    h_ref[...] = _rms_rows(xn, g_ref[...]).astype(h_ref.dtype)


def _mix_out(a, b, sg, x2, wa, wb, wo, g, tm=512):
    m, d = x2.shape
    return pl.pallas_call(
        _mix_out_kernel,
        grid=(m // tm,),
        in_specs=[
            pl.BlockSpec((tm, a.shape[1]), lambda i: (i, 0)),
            pl.BlockSpec((tm, b.shape[1]), lambda i: (i, 0)),
            pl.BlockSpec((tm, d), lambda i: (i, 0)),
            pl.BlockSpec((tm, d), lambda i: (i, 1)),
            pl.BlockSpec((tm, d), lambda i: (i, 0)),
            _resident(wa.shape), _resident(wb.shape), _resident(wo.shape), _resident(g.shape),
        ],
        out_specs=[pl.BlockSpec((tm, d), lambda i: (i, 0)), pl.BlockSpec((tm, d), lambda i: (i, 0))],
        out_shape=[jax.ShapeDtypeStruct((m, d), F32), jax.ShapeDtypeStruct((m, d), BF16)],
        compiler_params=_cparams(1),
        name="mix_out",
    )(a, b, sg, sg, x2, wa, wb, wo, g)


def _ffn_up_kernel(h_ref, wg_ref, wu_ref, cw_ref, cb_ref, o_ref, wgb_ref, wub_ref, tail_ref,
                   *, blocks_per_seq):
    i = pl.program_id(1)

    @pl.when(i == 0)
    def _():
        wgb_ref[...] = wg_ref[...].astype(BF16)
        wub_ref[...] = wu_ref[...].astype(BF16)

    @pl.when(i % blocks_per_seq == 0)
    def _():
        tail_ref[...] = jnp.zeros_like(tail_ref)

    h = h_ref[...]
    gate = _dot(h, wgb_ref[...])
    up = _dot(h, wub_ref[...])
    tm = gate.shape[0]
    ext = jnp.concatenate([tail_ref[...], gate], axis=0)
    tail_ref[...] = gate[tm - SUBLANES:, :]
    g_t1 = pltpu.roll(ext, 1, 0)[SUBLANES:, :]
    g_t2 = pltpu.roll(ext, 2, 0)[SUBLANES:, :]
    a = g_t2 * cw_ref[0:1, :]
    a = a + g_t1 * cw_ref[1:2, :]
    a = a + gate * cw_ref[2:3, :]
    a = a + cb_ref[...]
    o_ref[...] = (jax.nn.gelu(a) * up).astype(o_ref.dtype)


def _ffn_up(h, wg, wu, layer, cw, cb, seq, tm=1024, tf=512):
    m, d = h.shape
    f = wg.shape[2]
    kern = functools.partial(_ffn_up_kernel, blocks_per_seq=seq // tm)
    return pl.pallas_call(
        kern,
        grid=(f // tf, m // tm),
        in_specs=[
            pl.BlockSpec((tm, d), lambda j, i: (i, 0)),
            pl.BlockSpec((None, d, tf), lambda j, i: (layer, 0, j)),
            pl.BlockSpec((None, d, tf), lambda j, i: (layer, 0, j)),
            pl.BlockSpec((CONV_W, tf), lambda j, i: (0, j)),
            pl.BlockSpec((1, tf), lambda j, i: (0, j)),
        ],
        out_specs=pl.BlockSpec((tm, tf), lambda j, i: (i, j)),
        out_shape=jax.ShapeDtypeStruct((m, f), BF16),
        scratch_shapes=[pltpu.VMEM((d, tf), BF16), pltpu.VMEM((d, tf), BF16),
                        pltpu.VMEM((SUBLANES, tf), F32)],
        compiler_params=_cparams(2),
        name="ffn_up",
    )(h, wg, wu, cw, cb)


def _ffn_down_kernel(a_ref, x_ref, w_ref, o_ref):
    o_ref[...] = x_ref[...] + _dot(a_ref[...], w_ref[...])


def _ffn_down(act, x2, w, tm=512):
    m, d = x2.shape
    return pl.pallas_call(
        _ffn_down_kernel,
        grid=(m // tm,),
        in_specs=[
            pl.BlockSpec((tm, act.shape[1]), lambda i: (i, 0)),
            pl.BlockSpec((tm, d), lambda i: (i, 0)),
            _resident(w.shape),
        ],
        out_specs=pl.BlockSpec((tm, d), lambda i: (i, 0)),
        out_shape=jax.ShapeDtypeStruct((m, d), F32),
        compiler_params=_cparams(1),
        name="ffn_down",
    )(act, x2, w)


def _ple_kernel(x_ref, p_ref, g_ref, wg_ref, wp_ref, gn_ref, *out_refs):
    x = x_ref[...]
    h = _rms_rows(x, g_ref[...]).astype(BF16)
    gate = jax.nn.sigmoid(_dot(h, wg_ref[...]))
    proj = _dot(p_ref[...].astype(BF16), wp_ref[...])
    xn = x + proj * gate
    if len(out_refs) == 2:
        out_refs[0][...] = xn
    out_refs[-1][...] = _rms_rows(xn, gn_ref[...]).astype(out_refs[-1].dtype)


def _ple(x2, p2, g, wg, wp, g_next, last, tm=512):
    m, d = x2.shape
    row_spec = pl.BlockSpec((tm, d), lambda i: (i, 0))
    if last:
        out_specs, out_shape = [row_spec], [jax.ShapeDtypeStruct((m, d), F32)]
    else:
        out_specs = [row_spec, row_spec]
        out_shape = [jax.ShapeDtypeStruct((m, d), F32), jax.ShapeDtypeStruct((m, d), BF16)]
    return pl.pallas_call(
        _ple_kernel,
        grid=(m // tm,),
        in_specs=[
            row_spec,
            pl.BlockSpec((tm, PLE_DIM), lambda i: (i, 0)),
            _resident(g.shape), _resident(wg.shape), _resident(wp.shape), _resident(g_next.shape),
        ],
        out_specs=out_specs,
        out_shape=out_shape,
        compiler_params=_cparams(1),
        name="ple",
    )(x2, p2, g, wg, wp, g_next)


def _rotary_tables(seq):
    half = RET_DK // 2
    inv = ROPE_THETA ** (-jnp.arange(half, dtype=F32) / half)
    ang = jnp.arange(seq, dtype=jnp.int32).astype(F32)[:, None] * inv[None, :]
    cos = jnp.cos(ang)
    sin = jnp.sin(ang)
    return jnp.concatenate([cos, cos], axis=-1), jnp.concatenate([-sin, sin], axis=-1)


def _retention_tables():
    log_g = jnp.log1p(-jnp.exp2(-5.0 - jnp.arange(RET_HEADS, dtype=F32)))
    idx = jnp.arange(CHUNK, dtype=F32)
    diff = idx[:, None] - idx[None, :]
    causal = diff >= 0
    decay = jnp.where(causal[None], jnp.exp(log_g[:, None, None] * jnp.where(causal, diff, 0.0)[None]), 0.0)
    xi = jnp.exp(log_g[:, None] * (idx[None, :] + 1.0)).T
    zeta = jnp.exp(log_g[:, None] * (CHUNK - 1.0 - idx[None, :])).T
    chunk_decay = jnp.exp(log_g * CHUNK)
    xi_map = jnp.repeat(xi, RET_DV, axis=1)
    zeta_map = jnp.repeat(zeta, RET_DK, axis=1)
    cd = jnp.broadcast_to(chunk_decay[:, None, None], (RET_HEADS, 1, RET_DV))
    return decay, xi_map, zeta_map, cd


def kernel(x, p, mix_norm_g, w_in, sgu_norm_g, sgu_w, sgu_b, ret_norm_g, w_branch_a, w_branch_b, w_out, ffn_norm_g, ffn_w_gate, ffn_w_up, ffn_conv_w, ffn_conv_b, ffn_w_down, ple_norm_g, ple_w_gate, ple_w_proj, final_norm_g):
    batch, seq, d = x.shape
    depth = w_in.shape[0]
    m = batch * seq
    assert d == D_MODEL and seq % 1024 == 0 and w_in.shape[2] == IN_COLS

    cos_t, sin_t = _rotary_tables(seq)
    decay, xi_map, zeta_map, cd = _retention_tables()
    qk_scale = jnp.concatenate([jnp.ones((1, RET_QK_WIDTH), F32),
                                jnp.full((1, RET_QK_WIDTH), RET_DK ** -0.5, F32)], axis=1)
    rot = (cos_t, sin_t, qk_scale, seq)

    x2 = x.reshape(m, d)
    h = _norm_bf16(x2, mix_norm_g[0][None])
    for i in range(depth):
        uv, wa = _in_proj(h, w_in, i, COL_UV, 2 * SGU_WIDTH, jax.nn.gelu, sides=(w_branch_a,), sub=1024)
        qk, wb = _in_proj(h, w_in, i, COL_QK, 2 * RET_QK_WIDTH, None, sides=(w_branch_b,), rot=rot)
        rv, wo = _in_proj(h, w_in, i, COL_RV, RET_V_WIDTH, lambda z: z, sides=(w_out,), sub=1024)
        srg, wpg = _in_proj(h, w_in, i, COL_RG, RET_V_WIDTH, jax.nn.silu, sides=(ple_w_gate,))
        sg, wd = _in_proj(h, w_in, i, COL_G, 2 * D_MODEL, jax.nn.sigmoid, sides=(ffn_w_down,))
        sgu_bias = jnp.repeat(sgu_b[i].T, SGU_GROUP_DIM, axis=1)
        a, b = _mixer(uv, qk, rv, srg, sgu_norm_g[i][None], sgu_w[i], sgu_bias, ret_norm_g[i][None],
                      decay, xi_map, zeta_map, cd, batch, seq)
        x2, hf = _mix_out(a, b, sg, x2, wa, wb, wo, ffn_norm_g[i][None])
        act = _ffn_up(hf, ffn_w_gate, ffn_w_up, i, ffn_conv_w[i], ffn_conv_b[i][None], seq)
        x2 = _ffn_down(act, x2, wd)
        last = i == depth - 1
        g_next = final_norm_g[None] if last else mix_norm_g[i + 1][None]
        outs = _ple(x2, p[i].reshape(m, PLE_DIM), ple_norm_g[i][None], wpg, ple_w_proj[i].astype(BF16),
                    g_next, last)
        if last:
            (out,) = outs
        else:
            x2, h = outs
    return out.reshape(batch, seq, d)
```

```python
import functools

import jax
import jax.numpy as jnp
from jax import lax
from jax.experimental import pallas as pl
from jax.experimental.pallas import tpu as pltpu

D_MODEL = 2048
CHUNK = 128
SGU_GROUPS = 8
SGU_GROUP_DIM = 128
SGU_WIDTH = SGU_GROUPS * SGU_GROUP_DIM
RET_HEADS = 8
RET_DK = 128
RET_DV = 256
RET_QK_WIDTH = RET_HEADS * RET_DK
RET_V_WIDTH = RET_HEADS * RET_DV
ROPE_THETA = 10000.0
D_FF = 5632
CONV_W = 3
PLE_DIM = 256
NORM_EPS = 1e-6
IN_COLS = 2 * SGU_WIDTH + 2 * RET_QK_WIDTH + 2 * RET_V_WIDTH + 2 * D_MODEL

V7X_VMEM_LIMIT_BYTES = 62 * 1024 * 1024
SUBLANES = 8
BF16_ROWS = 16

BF16 = jnp.bfloat16
F32 = jnp.float32

COL_UV = 0
COL_QK = 2 * SGU_WIDTH
COL_RV = COL_QK + 2 * RET_QK_WIDTH
COL_RG = COL_RV + RET_V_WIDTH
COL_G = COL_RG + RET_V_WIDTH


def _cparams(n_axes):
    return pltpu.CompilerParams(
        dimension_semantics=("arbitrary",) * n_axes,
        vmem_limit_bytes=V7X_VMEM_LIMIT_BYTES,
    )


def _rms_rows(x, g):
    y = x * lax.rsqrt(jnp.mean(x * x, axis=-1, keepdims=True) + NORM_EPS)
    return y * g


def _dot(a, b):
    return jnp.dot(a, b, preferred_element_type=F32)


def _norm_kernel(x_ref, g_ref, o_ref):
    o_ref[...] = _rms_rows(x_ref[...], g_ref[...]).astype(o_ref.dtype)


def _norm_bf16(x2, g, tm=512):
    m, d = x2.shape
    return pl.pallas_call(
        _norm_kernel,
        grid=(m // tm,),
        in_specs=[pl.BlockSpec((tm, d), lambda i: (i, 0)), pl.BlockSpec((1, d), lambda i: (0, 0))],
        out_specs=pl.BlockSpec((tm, d), lambda i: (i, 0)),
        out_shape=jax.ShapeDtypeStruct((m, d), BF16),
        compiler_params=_cparams(1),
        name="norm",
    )(x2, g)


PROJ_SUB_ROWS = 256


def _cast_weight_once(w_ref, wb_ref):
    @pl.when(pl.program_id(1) == 0)
    def _():
        wb_ref[...] = w_ref[...].astype(BF16)


def _cast_sides(side_refs):
    n = len(side_refs) // 2
    for src, dst in zip(side_refs[:n], side_refs[n:]):
        dst[...] = src[...].astype(dst.dtype)


def _proj_kernel(h_ref, w_ref, *rest, act, n_side, sub):
    side_in, (o_ref,), side_out, (wb_ref,) = (rest[:n_side], rest[n_side:n_side + 1],
                                              rest[n_side + 1:2 * n_side + 1], rest[2 * n_side + 1:])
    _cast_weight_once(w_ref, wb_ref)
    _cast_sides(side_in + side_out)
    for r in range(h_ref.shape[0] // sub):
        rows = slice(r * sub, (r + 1) * sub)
        o_ref[rows, :] = act(_dot(h_ref[rows, :], wb_ref[...])).astype(o_ref.dtype)


def _rot_kernel(h_ref, w_ref, cos_ref, sin_ref, scale_ref, *rest, n_side, sub):
    side_in, (o_ref,), side_out, (wb_ref,) = (rest[:n_side], rest[n_side:n_side + 1],
                                              rest[n_side + 1:2 * n_side + 1], rest[2 * n_side + 1:])
    _cast_weight_once(w_ref, wb_ref)
    _cast_sides(side_in + side_out)
    for r in range(h_ref.shape[0] // sub):
        rows = slice(r * sub, (r + 1) * sub)
        z = _dot(h_ref[rows, :], wb_ref[...])
        cos = cos_ref[rows, :]
        sin = sin_ref[rows, :]
        for h in range(z.shape[1] // RET_DK):
            sl = slice(h * RET_DK, (h + 1) * RET_DK)
            zh = z[:, sl]
            rot = zh * cos + pltpu.roll(zh, RET_DK // 2, 1) * sin
            o_ref[rows, sl] = (rot * scale_ref[:, sl]).astype(o_ref.dtype)


def _side_specs(sides, layer, n_steps, inner):
    in_specs, out_specs, out_shapes = [], [], []
    for w in sides:
        _, rows, cols = w.shape
        slab = rows // n_steps
        assert slab * n_steps == rows and slab % BF16_ROWS == 0
        in_specs.append(pl.BlockSpec((None, slab, cols), lambda j, i: (layer, j * inner + i, 0)))
        out_specs.append(pl.BlockSpec((slab, cols), lambda j, i: (j * inner + i, 0)))
        out_shapes.append(jax.ShapeDtypeStruct((rows, cols), BF16))
    return in_specs, out_specs, out_shapes


def _in_proj(h, w, layer, col0, n, act, sides=(), rot=None, tm=2048, tn=1024, sub=PROJ_SUB_ROWS):
    m, d = h.shape
    off = col0 // tn
    grid = (n // tn, m // tm)
    s_in, s_out, s_shapes = _side_specs(sides, layer, grid[0] * grid[1], grid[1])
    in_specs = [
        pl.BlockSpec((tm, d), lambda j, i: (i, 0)),
        pl.BlockSpec((None, d, tn), lambda j, i: (layer, 0, off + j)),
    ]
    args = [h, w]
    if rot is None:
        kern = functools.partial(_proj_kernel, act=act, n_side=len(sides), sub=sub)
    else:
        cos_t, sin_t, scale, seq = rot
        pos_blocks = seq // tm
        in_specs += [
            pl.BlockSpec((tm, RET_DK), lambda j, i: (i % pos_blocks, 0)),
            pl.BlockSpec((tm, RET_DK), lambda j, i: (i % pos_blocks, 0)),
            pl.BlockSpec((1, tn), lambda j, i: (0, j)),
        ]
        args += [cos_t, sin_t, scale]
        kern = functools.partial(_rot_kernel, n_side=len(sides), sub=sub)
    return pl.pallas_call(
        kern,
        grid=grid,
        in_specs=in_specs + s_in,
        out_specs=[pl.BlockSpec((tm, tn), lambda j, i: (i, j))] + s_out,
        out_shape=[jax.ShapeDtypeStruct((m, n), BF16)] + s_shapes,
        scratch_shapes=[pltpu.VMEM((d, tn), BF16)],
        compiler_params=_cparams(2),
        name="in_proj_rot" if rot is not None else "in_proj",
    )(*args, *sides)


MIXER_CHUNKS_PER_STEP = 2


def _sgu_pair(gu_ref, gv_ref, sgu_g_ref, sgu_w_ref, sgu_bias_ref, a_ref):
    row = lax.broadcasted_iota(jnp.int32, (CHUNK, CHUNK), 0)
    col = lax.broadcasted_iota(jnp.int32, (CHUNK, CHUNK), 1)
    causal = row >= col
    chunks = [slice(c * CHUNK, (c + 1) * CHUNK) for c in range(MIXER_CHUNKS_PER_STEP)]
    vn = [_rms_rows(gv_ref[rows, :].astype(F32), sgu_g_ref[...]).astype(BF16) for rows in chunks]
    for g in range(SGU_GROUPS):
        sl = slice(g * SGU_GROUP_DIM, (g + 1) * SGU_GROUP_DIM)
        w = jnp.where(causal, sgu_w_ref[g], 0.0).astype(BF16)
        s = _dot(w, jnp.concatenate([v[:, sl] for v in vn], axis=1))
        for c, rows in enumerate(chunks):
            s_c = s[:, c * SGU_GROUP_DIM:(c + 1) * SGU_GROUP_DIM] + sgu_bias_ref[:, sl]
            a_ref[rows, sl] = (gu_ref[rows, sl].astype(F32) * s_c).astype(a_ref.dtype)


def _retention_scores(c, q_ref, k_ref, rv_ref, xi_ref, zeta_ref, state_ref):
    rows = slice(c * CHUNK, (c + 1) * CHUNK)
    out = []
    for h in range(RET_HEADS):
        ksl = slice(h * RET_DK, (h + 1) * RET_DK)
        vsl = slice(h * RET_DV, (h + 1) * RET_DV)
        q = q_ref[rows, ksl]
        k = k_ref[rows, ksl]
        v = rv_ref[rows, vsl]
        scores = lax.dot_general(q, k, (((1,), (1,)), ((), ())), preferred_element_type=F32)
        kz = (k.astype(F32) * zeta_ref[:, ksl]).astype(BF16)
        prev = state_ref[h]
        q_xi = (q.astype(F32) * xi_ref[:, h * RET_DV:h * RET_DV + RET_DK]).astype(BF16)
        out.append((scores, kz, prev, q_xi, v))
    return out


def _retention_outputs(c, stage1, srg_ref, ret_g_ref, decay_ref, cd_ref, b_ref, state_ref):
    rows = slice(c * CHUNK, (c + 1) * CHUNK)
    for h, (scores, kz, prev, q_xi, v) in enumerate(stage1):
        vsl = slice(h * RET_DV, (h + 1) * RET_DV)
        lhs = jnp.concatenate([(scores * decay_ref[h]).astype(BF16), q_xi], axis=1)
        rhs = jnp.concatenate([v, prev.astype(BF16)], axis=0)
        y = _dot(lhs, rhs)
        kv = lax.dot_general(kz, v, (((0,), (0,)), ((), ())), preferred_element_type=F32)
        state_ref[h] = prev * cd_ref[h] + kv
        y = _rms_rows(y, ret_g_ref[:, vsl])
        b_ref[rows, vsl] = (srg_ref[rows, vsl].astype(F32) * y).astype(b_ref.dtype)


def _gate_mixer_kernel(h_ref, w_ref, wd_ref, gu_ref, gv_ref, q_ref, k_ref, rv_ref, srg_ref,
                       sgu_g_ref, sgu_w_ref, sgu_bias_ref, ret_g_ref, decay_ref, xi_ref, zeta_ref, cd_ref,
                       o_ref, wdo_ref, a_ref, b_ref, wb_ref, state_ref, *, steps_per_seq, sub):
    step = pl.program_id(0) * pl.num_programs(1) + pl.program_id(1)
    _cast_weight_once(w_ref, wb_ref)

    @pl.when(step % steps_per_seq == 0)
    def _():
        state_ref[...] = jnp.zeros_like(state_ref)

    wdo_ref[...] = wd_ref[...].astype(wdo_ref.dtype)

    def gate_slab(r):
        rows = slice(r * sub, (r + 1) * sub)
        o_ref[rows, :] = jax.nn.sigmoid(_dot(h_ref[rows, :], wb_ref[...])).astype(o_ref.dtype)

    assert h_ref.shape[0] // sub == 4 and MIXER_CHUNKS_PER_STEP == 2
    stage1 = _retention_scores(0, q_ref, k_ref, rv_ref, xi_ref, zeta_ref, state_ref)
    gate_slab(0)
    _sgu_pair(gu_ref, gv_ref, sgu_g_ref, sgu_w_ref, sgu_bias_ref, a_ref)
    _retention_outputs(0, stage1, srg_ref, ret_g_ref, decay_ref, cd_ref, b_ref, state_ref)
    gate_slab(1)
    stage1 = _retention_scores(1, q_ref, k_ref, rv_ref, xi_ref, zeta_ref, state_ref)
    gate_slab(2)
    _retention_outputs(1, stage1, srg_ref, ret_g_ref, decay_ref, cd_ref, b_ref, state_ref)
    gate_slab(3)


def _gate_mixer(h, w, layer, col0, n, w_down, uv, qk, rv, srg, sgu_g, sgu_w, sgu_bias, ret_g,
                decay, xi_map, zeta_map, cd, seq, tm=1024, tn=1024, sub=PROJ_SUB_ROWS):
    m, d = h.shape
    off = col0 // tn
    grid = (n // tn, m // tm)
    inner = grid[1]
    mix_rows = MIXER_CHUNKS_PER_STEP * CHUNK
    assert grid[0] * grid[1] * mix_rows == m and seq % mix_rows == 0
    (wd_in,), (wd_out,), (wd_shape,) = _side_specs((w_down,), layer, grid[0] * grid[1], inner)

    def rows(width, blk):
        return pl.BlockSpec((mix_rows, width), lambda j, i: (j * inner + i, blk))

    def const(a):
        return pl.BlockSpec(a.shape, lambda j, i: (0,) * a.ndim, pipeline_mode=pl.Buffered(1))

    consts = (sgu_g, sgu_w, sgu_bias, ret_g, decay, xi_map, zeta_map, cd)
    kern = functools.partial(_gate_mixer_kernel, steps_per_seq=seq // mix_rows, sub=sub)
    return pl.pallas_call(
        kern,
        grid=grid,
        in_specs=[
            pl.BlockSpec((tm, d), lambda j, i: (i, 0)),
            pl.BlockSpec((None, d, tn), lambda j, i: (layer, 0, off + j)),
            wd_in,
            rows(SGU_WIDTH, 0), rows(SGU_WIDTH, 1),
            rows(RET_QK_WIDTH, 0), rows(RET_QK_WIDTH, 1),
            rows(RET_V_WIDTH, 0), rows(RET_V_WIDTH, 0),
        ] + [const(c) for c in consts],
        out_specs=[pl.BlockSpec((tm, tn), lambda j, i: (i, j)), wd_out,
                   rows(SGU_WIDTH, 0), rows(RET_V_WIDTH, 0)],
        out_shape=[jax.ShapeDtypeStruct((m, n), BF16), wd_shape,
                   jax.ShapeDtypeStruct((m, SGU_WIDTH), BF16),
                   jax.ShapeDtypeStruct((m, RET_V_WIDTH), BF16)],
        scratch_shapes=[pltpu.VMEM((d, tn), BF16), pltpu.VMEM((RET_HEADS, RET_DK, RET_DV), F32)],
        compiler_params=_cparams(2),
        name="gate_mixer",
    )(h, w, w_down, uv, uv, qk, qk, rv, srg, *consts)


def _resident(shape):
    return pl.BlockSpec(shape, lambda *_: (0,) * len(shape), pipeline_mode=pl.Buffered(1))


def _mix_out_kernel(a_ref, b_ref, sga_ref, sgb_ref, x_ref, wa_ref, wb_ref, wo_ref, g_ref, xo_ref, h_ref):
    ya = _dot(a_ref[...], wa_ref[...])
    yb = _dot(b_ref[...], wb_ref[...])
    m = (sga_ref[...].astype(F32) * ya + sgb_ref[...].astype(F32) * yb).astype(BF16)
    xn = x_ref[...] + _dot(m, wo_ref[...])
    xo_ref[...] = xn
    h_ref[...] = _rms_rows(xn, g_ref[...]).astype(h_ref.dtype)


def _mix_out(a, b, sg, x2, wa, wb, wo, g, tm=512):
    m, d = x2.shape
    return pl.pallas_call(
        _mix_out_kernel,
        grid=(m // tm,),
        in_specs=[
            pl.BlockSpec((tm, a.shape[1]), lambda i: (i, 0)),
            pl.BlockSpec((tm, b.shape[1]), lambda i: (i, 0)),
            pl.BlockSpec((tm, d), lambda i: (i, 0)),
            pl.BlockSpec((tm, d), lambda i: (i, 1)),
            pl.BlockSpec((tm, d), lambda i: (i, 0)),
            _resident(wa.shape), _resident(wb.shape), _resident(wo.shape), _resident(g.shape),
        ],
        out_specs=[pl.BlockSpec((tm, d), lambda i: (i, 0)), pl.BlockSpec((tm, d), lambda i: (i, 0))],
        out_shape=[jax.ShapeDtypeStruct((m, d), F32), jax.ShapeDtypeStruct((m, d), BF16)],
        compiler_params=_cparams(1),
        name="mix_out",
    )(a, b, sg, sg, x2, wa, wb, wo, g)


def _ffn_up_kernel(h_ref, wg_ref, wu_ref, cw_ref, cb_ref, o_ref, wgb_ref, wub_ref, tail_ref,
                   *, blocks_per_seq):
    i = pl.program_id(1)

    @pl.when(i == 0)
    def _():
        wgb_ref[...] = wg_ref[...].astype(BF16)
        wub_ref[...] = wu_ref[...].astype(BF16)

    @pl.when(i % blocks_per_seq == 0)
    def _():
        tail_ref[...] = jnp.zeros_like(tail_ref)

    h = h_ref[...]
    gate = _dot(h, wgb_ref[...])
    up = _dot(h, wub_ref[...])
    tm = gate.shape[0]
    ext = jnp.concatenate([tail_ref[...], gate], axis=0)
    tail_ref[...] = gate[tm - SUBLANES:, :]
    g_t1 = pltpu.roll(ext, 1, 0)[SUBLANES:, :]
    g_t2 = pltpu.roll(ext, 2, 0)[SUBLANES:, :]
    a = g_t2 * cw_ref[0:1, :]
    a = a + g_t1 * cw_ref[1:2, :]
    a = a + gate * cw_ref[2:3, :]
    a = a + cb_ref[...]
    o_ref[...] = (jax.nn.gelu(a) * up).astype(o_ref.dtype)


def _ffn_up(h, wg, wu, layer, cw, cb, seq, tm=1024, tf=512):
    m, d = h.shape
    f = wg.shape[2]
    kern = functools.partial(_ffn_up_kernel, blocks_per_seq=seq // tm)
    return pl.pallas_call(
        kern,
        grid=(f // tf, m // tm),
        in_specs=[
            pl.BlockSpec((tm, d), lambda j, i: (i, 0)),
            pl.BlockSpec((None, d, tf), lambda j, i: (layer, 0, j)),
            pl.BlockSpec((None, d, tf), lambda j, i: (layer, 0, j)),
            pl.BlockSpec((CONV_W, tf), lambda j, i: (0, j)),
            pl.BlockSpec((1, tf), lambda j, i: (0, j)),
        ],
        out_specs=pl.BlockSpec((tm, tf), lambda j, i: (i, j)),
        out_shape=jax.ShapeDtypeStruct((m, f), BF16),
        scratch_shapes=[pltpu.VMEM((d, tf), BF16), pltpu.VMEM((d, tf), BF16),
                        pltpu.VMEM((SUBLANES, tf), F32)],
        compiler_params=_cparams(2),
        name="ffn_up",
    )(h, wg, wu, cw, cb)


def _ffn_down_kernel(a_ref, x_ref, w_ref, o_ref):
    o_ref[...] = x_ref[...] + _dot(a_ref[...], w_ref[...])


def _ffn_down(act, x2, w, tm=512):
    m, d = x2.shape
    return pl.pallas_call(
        _ffn_down_kernel,
        grid=(m // tm,),
        in_specs=[
            pl.BlockSpec((tm, act.shape[1]), lambda i: (i, 0)),
            pl.BlockSpec((tm, d), lambda i: (i, 0)),
            _resident(w.shape),
        ],
        out_specs=pl.BlockSpec((tm, d), lambda i: (i, 0)),
        out_shape=jax.ShapeDtypeStruct((m, d), F32),
        compiler_params=_cparams(1),
        name="ffn_down",
    )(act, x2, w)


def _ple_kernel(x_ref, p_ref, g_ref, wg_ref, wp_ref, gn_ref, *out_refs):
    x = x_ref[...]
    h = _rms_rows(x, g_ref[...]).astype(BF16)
    gate = jax.nn.sigmoid(_dot(h, wg_ref[...]))
    proj = _dot(p_ref[...].astype(BF16), wp_ref[...])
    xn = x + proj * gate
    if len(out_refs) == 2:
        out_refs[0][...] = xn
    out_refs[-1][...] = _rms_rows(xn, gn_ref[...]).astype(out_refs[-1].dtype)


def _ple(x2, p2, g, wg, wp, g_next, last, tm=512):
    m, d = x2.shape
    row_spec = pl.BlockSpec((tm, d), lambda i: (i, 0))
    if last:
        out_specs, out_shape = [row_spec], [jax.ShapeDtypeStruct((m, d), F32)]
    else:
        out_specs = [row_spec, row_spec]
        out_shape = [jax.ShapeDtypeStruct((m, d), F32), jax.ShapeDtypeStruct((m, d), BF16)]
    return pl.pallas_call(
        _ple_kernel,
        grid=(m // tm,),
        in_specs=[
            row_spec,
            pl.BlockSpec((tm, PLE_DIM), lambda i: (i, 0)),
            _resident(g.shape), _resident(wg.shape), _resident(wp.shape), _resident(g_next.shape),
        ],
        out_specs=out_specs,
        out_shape=out_shape,
        compiler_params=_cparams(1),
        name="ple",
    )(x2, p2, g, wg, wp, g_next)


def _rotary_tables(seq):
    half = RET_DK // 2
    inv = ROPE_THETA ** (-jnp.arange(half, dtype=F32) / half)
    ang = jnp.arange(seq, dtype=jnp.int32).astype(F32)[:, None] * inv[None, :]
    cos = jnp.cos(ang)
    sin = jnp.sin(ang)
    return jnp.concatenate([cos, cos], axis=-1), jnp.concatenate([-sin, sin], axis=-1)


def _retention_tables():
    log_g = jnp.log1p(-jnp.exp2(-5.0 - jnp.arange(RET_HEADS, dtype=F32)))
    idx = jnp.arange(CHUNK, dtype=F32)
    diff = idx[:, None] - idx[None, :]
    causal = diff >= 0
    decay = jnp.where(causal[None], jnp.exp(log_g[:, None, None] * jnp.where(causal, diff, 0.0)[None]), 0.0)
    xi = jnp.exp(log_g[:, None] * (idx[None, :] + 1.0)).T
    zeta = jnp.exp(log_g[:, None] * (CHUNK - 1.0 - idx[None, :])).T
    chunk_decay = jnp.exp(log_g * CHUNK)
    xi_map = jnp.repeat(xi, RET_DV, axis=1)
    zeta_map = jnp.repeat(zeta, RET_DK, axis=1)
    cd = jnp.broadcast_to(chunk_decay[:, None, None], (RET_HEADS, 1, RET_DV))
    return decay, xi_map, zeta_map, cd


def kernel(x, p, mix_norm_g, w_in, sgu_norm_g, sgu_w, sgu_b, ret_norm_g, w_branch_a, w_branch_b, w_out, ffn_norm_g, ffn_w_gate, ffn_w_up, ffn_conv_w, ffn_conv_b, ffn_w_down, ple_norm_g, ple_w_gate, ple_w_proj, final_norm_g):
    batch, seq, d = x.shape
    depth = w_in.shape[0]
    m = batch * seq
    assert d == D_MODEL and seq % 1024 == 0 and w_in.shape[2] == IN_COLS

    cos_t, sin_t = _rotary_tables(seq)
    decay, xi_map, zeta_map, cd = _retention_tables()
    qk_scale = jnp.concatenate([jnp.ones((1, RET_QK_WIDTH), F32),
                                jnp.full((1, RET_QK_WIDTH), RET_DK ** -0.5, F32)], axis=1)
    rot = (cos_t, sin_t, qk_scale, seq)

    x2 = x.reshape(m, d)
    h = _norm_bf16(x2, mix_norm_g[0][None])
    for i in range(depth):
        uv, wa = _in_proj(h, w_in, i, COL_UV, 2 * SGU_WIDTH, jax.nn.gelu, sides=(w_branch_a,), sub=1024)
        qk, wb = _in_proj(h, w_in, i, COL_QK, 2 * RET_QK_WIDTH, None, sides=(w_branch_b,), rot=rot)
        rv, wo = _in_proj(h, w_in, i, COL_RV, RET_V_WIDTH, lambda z: z, sides=(w_out,), sub=1024)
        srg, wpg = _in_proj(h, w_in, i, COL_RG, RET_V_WIDTH, jax.nn.silu, sides=(ple_w_gate,))
        sgu_bias = jnp.repeat(sgu_b[i].T, SGU_GROUP_DIM, axis=1)
        sg, wd, a, b = _gate_mixer(h, w_in, i, COL_G, 2 * D_MODEL, ffn_w_down, uv, qk, rv, srg,
                                   sgu_norm_g[i][None], sgu_w[i], sgu_bias, ret_norm_g[i][None],
                                   decay, xi_map, zeta_map, cd, seq)
        x2, hf = _mix_out(a, b, sg, x2, wa, wb, wo, ffn_norm_g[i][None])
        act = _ffn_up(hf, ffn_w_gate, ffn_w_up, i, ffn_conv_w[i], ffn_conv_b[i][None], seq)
        x2 = _ffn_down(act, x2, wd)
        last = i == depth - 1
        g_next = final_norm_g[None] if last else mix_norm_g[i + 1][None]
        outs = _ple(x2, p[i].reshape(m, PLE_DIM), ple_norm_g[i][None], wpg, ple_w_proj[i].astype(BF16),
                    g_next, last)
        if last:
            (out,) = outs
        else:
            x2, h = outs
    return out.reshape(batch, seq, d)
```

```python
import functools

import jax
import jax.numpy as jnp
from jax import lax
from jax.experimental import pallas as pl
from jax.experimental.pallas import tpu as pltpu

D_MODEL = 2048
CHUNK = 128
SGU_GROUPS = 8
SGU_GROUP_DIM = 128
SGU_WIDTH = SGU_GROUPS * SGU_GROUP_DIM
RET_HEADS = 8
RET_DK = 128
RET_DV = 256
RET_QK_WIDTH = RET_HEADS * RET_DK
RET_V_WIDTH = RET_HEADS * RET_DV
ROPE_THETA = 10000.0
D_FF = 5632
CONV_W = 3
PLE_DIM = 256
NORM_EPS = 1e-6
IN_COLS = 2 * SGU_WIDTH + 2 * RET_QK_WIDTH + 2 * RET_V_WIDTH + 2 * D_MODEL

V7X_VMEM_LIMIT_BYTES = 62 * 1024 * 1024
SUBLANES = 8
BF16_ROWS = 16

BF16 = jnp.bfloat16
F32 = jnp.float32

COL_UV = 0
COL_QK = 2 * SGU_WIDTH
COL_RV = COL_QK + 2 * RET_QK_WIDTH
COL_RG = COL_RV + RET_V_WIDTH
COL_G = COL_RG + RET_V_WIDTH


def _cparams(n_axes):
    return pltpu.CompilerParams(
        dimension_semantics=("arbitrary",) * n_axes,
        vmem_limit_bytes=V7X_VMEM_LIMIT_BYTES,
    )


def _rms_rows(x, g):
    y = x * lax.rsqrt(jnp.mean(x * x, axis=-1, keepdims=True) + NORM_EPS)
    return y * g


def _dot(a, b):
    return jnp.dot(a, b, preferred_element_type=F32)


def _norm_kernel(x_ref, g_ref, o_ref):
    o_ref[...] = _rms_rows(x_ref[...], g_ref[...]).astype(o_ref.dtype)


def _norm_bf16(x2, g, tm=512):
    m, d = x2.shape
    return pl.pallas_call(
        _norm_kernel,
        grid=(m // tm,),
        in_specs=[pl.BlockSpec((tm, d), lambda i: (i, 0)), pl.BlockSpec((1, d), lambda i: (0, 0))],
        out_specs=pl.BlockSpec((tm, d), lambda i: (i, 0)),
        out_shape=jax.ShapeDtypeStruct((m, d), BF16),
        compiler_params=_cparams(1),
        name="norm",
    )(x2, g)


PROJ_SUB_ROWS = 256


def _cast_weight_once(w_ref, wb_ref):
    @pl.when(pl.program_id(1) == 0)
    def _():
        wb_ref[...] = w_ref[...].astype(BF16)


def _cast_sides(side_refs):
    n = len(side_refs) // 2
    for src, dst in zip(side_refs[:n], side_refs[n:]):
        dst[...] = src[...].astype(dst.dtype)


def _side_specs(sides, layer, n_steps, inner):
    in_specs, out_specs, out_shapes = [], [], []
    for w in sides:
        _, rows, cols = w.shape
        slab = rows // n_steps
        assert slab * n_steps == rows and slab % BF16_ROWS == 0
        in_specs.append(pl.BlockSpec((None, slab, cols), lambda j, i: (layer, j * inner + i, 0)))
        out_specs.append(pl.BlockSpec((slab, cols), lambda j, i: (j * inner + i, 0)))
        out_shapes.append(jax.ShapeDtypeStruct((rows, cols), BF16))
    return in_specs, out_specs, out_shapes


PROJ_TN = 1024
PROJ_KIND_BLOCKS = 2
N_PROJ_SIDES = 4


def _proj_kernel(h_ref, w_ref, cos_ref, sin_ref, scale_ref, *rest):
    n = N_PROJ_SIDES
    side_in, o_ref, side_out, wb_ref = rest[:n], rest[n], rest[n + 1:2 * n + 1], rest[2 * n + 1]
    kind = pl.program_id(0) // PROJ_KIND_BLOCKS
    _cast_weight_once(w_ref, wb_ref)
    _cast_sides(side_in + side_out)

    def slabs(sub, act):
        for r in range(h_ref.shape[0] // sub):
            rows = slice(r * sub, (r + 1) * sub)
            act(rows, _dot(h_ref[rows, :], wb_ref[...]))

    def store(fn):
        def act(rows, z):
            o_ref[rows, :] = fn(z).astype(o_ref.dtype)
        return act

    def rotary(rows, z):
        cos = cos_ref[rows, :]
        sin = sin_ref[rows, :]
        for h in range(z.shape[1] // RET_DK):
            sl = slice(h * RET_DK, (h + 1) * RET_DK)
            zh = z[:, sl]
            rot = zh * cos + pltpu.roll(zh, RET_DK // 2, 1) * sin
            o_ref[rows, sl] = (rot * scale_ref[:, sl]).astype(o_ref.dtype)

    @pl.when(kind == 0)
    def _():
        slabs(4 * PROJ_SUB_ROWS, store(jax.nn.gelu))

    @pl.when(kind == 1)
    def _():
        slabs(PROJ_SUB_ROWS, rotary)

    @pl.when(kind == 2)
    def _():
        slabs(4 * PROJ_SUB_ROWS, store(lambda z: z))

    @pl.when(kind == 3)
    def _():
        slabs(PROJ_SUB_ROWS, store(jax.nn.silu))


def _in_proj(h, w, layer, sides, cos_t, sin_t, scale, seq, tm=2048):
    m, d = h.shape
    tn = PROJ_TN
    grid = (4 * PROJ_KIND_BLOCKS, m // tm)
    n = grid[0] * tn
    assert len(sides) == N_PROJ_SIDES
    s_in, s_out, s_shapes = _side_specs(sides, layer, grid[0] * grid[1], grid[1])
    pos_blocks = seq // tm
    rot_block = lambda j: jnp.clip(j - PROJ_KIND_BLOCKS, 0, PROJ_KIND_BLOCKS - 1)
    return pl.pallas_call(
        _proj_kernel,
        grid=grid,
        in_specs=[
            pl.BlockSpec((tm, d), lambda j, i: (i, 0)),
            pl.BlockSpec((None, d, tn), lambda j, i: (layer, 0, j)),
            pl.BlockSpec((tm, RET_DK), lambda j, i: (i % pos_blocks, 0)),
            pl.BlockSpec((tm, RET_DK), lambda j, i: (i % pos_blocks, 0)),
            pl.BlockSpec((1, tn), lambda j, i: (0, rot_block(j))),
        ] + s_in,
        out_specs=[pl.BlockSpec((tm, tn), lambda j, i: (i, j))] + s_out,
        out_shape=[jax.ShapeDtypeStruct((m, n), BF16)] + s_shapes,
        scratch_shapes=[pltpu.VMEM((d, tn), BF16)],
        compiler_params=_cparams(2),
        name="in_proj",
    )(h, w, cos_t, sin_t, scale, *sides)


MIXER_CHUNKS_PER_STEP = 2


def _sgu_pair(gu_ref, gv_ref, sgu_g_ref, sgu_w_ref, sgu_bias_ref, a_ref):
    row = lax.broadcasted_iota(jnp.int32, (CHUNK, CHUNK), 0)
    col = lax.broadcasted_iota(jnp.int32, (CHUNK, CHUNK), 1)
    causal = row >= col
    chunks = [slice(c * CHUNK, (c + 1) * CHUNK) for c in range(MIXER_CHUNKS_PER_STEP)]
    vn = [_rms_rows(gv_ref[rows, :].astype(F32), sgu_g_ref[...]).astype(BF16) for rows in chunks]
    for g in range(SGU_GROUPS):
        sl = slice(g * SGU_GROUP_DIM, (g + 1) * SGU_GROUP_DIM)
        w = jnp.where(causal, sgu_w_ref[g], 0.0).astype(BF16)
        s = _dot(w, jnp.concatenate([v[:, sl] for v in vn], axis=1))
        for c, rows in enumerate(chunks):
            s_c = s[:, c * SGU_GROUP_DIM:(c + 1) * SGU_GROUP_DIM] + sgu_bias_ref[:, sl]
            a_ref[rows, sl] = (gu_ref[rows, sl].astype(F32) * s_c).astype(a_ref.dtype)


def _retention_scores(c, q_ref, k_ref, rv_ref, xi_ref, zeta_ref, state_ref):
    rows = slice(c * CHUNK, (c + 1) * CHUNK)
    out = []
    for h in range(RET_HEADS):
        ksl = slice(h * RET_DK, (h + 1) * RET_DK)
        vsl = slice(h * RET_DV, (h + 1) * RET_DV)
        q = q_ref[rows, ksl]
        k = k_ref[rows, ksl]
        v = rv_ref[rows, vsl]
        scores = lax.dot_general(q, k, (((1,), (1,)), ((), ())), preferred_element_type=F32)
        kz = (k.astype(F32) * zeta_ref[:, ksl]).astype(BF16)
        prev = state_ref[h]
        q_xi = (q.astype(F32) * xi_ref[:, h * RET_DV:h * RET_DV + RET_DK]).astype(BF16)
        out.append((scores, kz, prev, q_xi, v))
    return out


def _retention_outputs(c, stage1, srg_ref, ret_g_ref, decay_ref, cd_ref, b_ref, state_ref):
    rows = slice(c * CHUNK, (c + 1) * CHUNK)
    for h, (scores, kz, prev, q_xi, v) in enumerate(stage1):
        vsl = slice(h * RET_DV, (h + 1) * RET_DV)
        lhs = jnp.concatenate([(scores * decay_ref[h]).astype(BF16), q_xi], axis=1)
        rhs = jnp.concatenate([v, prev.astype(BF16)], axis=0)
        y = _dot(lhs, rhs)
        kv = lax.dot_general(kz, v, (((0,), (0,)), ((), ())), preferred_element_type=F32)
        state_ref[h] = prev * cd_ref[h] + kv
        y = _rms_rows(y, ret_g_ref[:, vsl])
        b_ref[rows, vsl] = (srg_ref[rows, vsl].astype(F32) * y).astype(b_ref.dtype)


def _gate_mixer_kernel(h_ref, w_ref, wd_ref, gu_ref, gv_ref, q_ref, k_ref, rv_ref, srg_ref,
                       sgu_g_ref, sgu_w_ref, sgu_bias_ref, ret_g_ref, decay_ref, xi_ref, zeta_ref, cd_ref,
                       o_ref, wdo_ref, a_ref, b_ref, wb_ref, state_ref, *, steps_per_seq, sub):
    step = pl.program_id(0) * pl.num_programs(1) + pl.program_id(1)
    _cast_weight_once(w_ref, wb_ref)

    @pl.when(step % steps_per_seq == 0)
    def _():
        state_ref[...] = jnp.zeros_like(state_ref)

    wdo_ref[...] = wd_ref[...].astype(wdo_ref.dtype)

    def gate_slab(r):
        rows = slice(r * sub, (r + 1) * sub)
        o_ref[rows, :] = jax.nn.sigmoid(_dot(h_ref[rows, :], wb_ref[...])).astype(o_ref.dtype)

    assert h_ref.shape[0] // sub == 4 and MIXER_CHUNKS_PER_STEP == 2
    stage1 = _retention_scores(0, q_ref, k_ref, rv_ref, xi_ref, zeta_ref, state_ref)
    gate_slab(0)
    _sgu_pair(gu_ref, gv_ref, sgu_g_ref, sgu_w_ref, sgu_bias_ref, a_ref)
    _retention_outputs(0, stage1, srg_ref, ret_g_ref, decay_ref, cd_ref, b_ref, state_ref)
    gate_slab(1)
    stage1 = _retention_scores(1, q_ref, k_ref, rv_ref, xi_ref, zeta_ref, state_ref)
    gate_slab(2)
    _retention_outputs(1, stage1, srg_ref, ret_g_ref, decay_ref, cd_ref, b_ref, state_ref)
    gate_slab(3)


def _gate_mixer(h, w, layer, col0, n, w_down, z, sgu_g, sgu_w, sgu_bias, ret_g,
                decay, xi_map, zeta_map, cd, seq, tm=1024, tn=1024, sub=PROJ_SUB_ROWS):
    m, d = h.shape
    off = col0 // tn
    grid = (n // tn, m // tm)
    inner = grid[1]
    mix_rows = MIXER_CHUNKS_PER_STEP * CHUNK
    assert grid[0] * grid[1] * mix_rows == m and seq % mix_rows == 0
    (wd_in,), (wd_out,), (wd_shape,) = _side_specs((w_down,), layer, grid[0] * grid[1], inner)

    def rows(width, blk):
        return pl.BlockSpec((mix_rows, width), lambda j, i: (j * inner + i, blk))

    def const(a):
        return pl.BlockSpec(a.shape, lambda j, i: (0,) * a.ndim, pipeline_mode=pl.Buffered(1))

    consts = (sgu_g, sgu_w, sgu_bias, ret_g, decay, xi_map, zeta_map, cd)
    kern = functools.partial(_gate_mixer_kernel, steps_per_seq=seq // mix_rows, sub=sub)
    return pl.pallas_call(
        kern,
        grid=grid,
        in_specs=[
            pl.BlockSpec((tm, d), lambda j, i: (i, 0)),
            pl.BlockSpec((None, d, tn), lambda j, i: (layer, 0, off + j)),
            wd_in,
            rows(SGU_WIDTH, COL_UV // SGU_WIDTH), rows(SGU_WIDTH, COL_UV // SGU_WIDTH + 1),
            rows(RET_QK_WIDTH, COL_QK // RET_QK_WIDTH), rows(RET_QK_WIDTH, COL_QK // RET_QK_WIDTH + 1),
            rows(RET_V_WIDTH, COL_RV // RET_V_WIDTH), rows(RET_V_WIDTH, COL_RG // RET_V_WIDTH),
        ] + [const(c) for c in consts],
        out_specs=[pl.BlockSpec((tm, tn), lambda j, i: (i, j)), wd_out,
                   rows(SGU_WIDTH, 0), rows(RET_V_WIDTH, 0)],
        out_shape=[jax.ShapeDtypeStruct((m, n), BF16), wd_shape,
                   jax.ShapeDtypeStruct((m, SGU_WIDTH), BF16),
                   jax.ShapeDtypeStruct((m, RET_V_WIDTH), BF16)],
        scratch_shapes=[pltpu.VMEM((d, tn), BF16), pltpu.VMEM((RET_HEADS, RET_DK, RET_DV), F32)],
        compiler_params=_cparams(2),
        name="gate_mixer",
    )(h, w, w_down, z, z, z, z, z, z, *consts)


def _resident(shape):
    return pl.BlockSpec(shape, lambda *_: (0,) * len(shape), pipeline_mode=pl.Buffered(1))


def _mix_out_kernel(a_ref, b_ref, sga_ref, sgb_ref, x_ref, wa_ref, wb_ref, wo_ref, g_ref, xo_ref, h_ref):
    ya = _dot(a_ref[...], wa_ref[...])
    yb = _dot(b_ref[...], wb_ref[...])
    m = (sga_ref[...].astype(F32) * ya + sgb_ref[...].astype(F32) * yb).astype(BF16)
    xn = x_ref[...] + _dot(m, wo_ref[...])
    xo_ref[...] = xn
    h_ref[...] = _rms_rows(xn, g_ref[...]).astype(h_ref.dtype)


def _mix_out(a, b, sg, x2, wa, wb, wo, g, tm=512):
    m, d = x2.shape
    return pl.pallas_call(
        _mix_out_kernel,
        grid=(m // tm,),
        in_specs=[
            pl.BlockSpec((tm, a.shape[1]), lambda i: (i, 0)),
            pl.BlockSpec((tm, b.shape[1]), lambda i: (i, 0)),
            pl.BlockSpec((tm, d), lambda i: (i, 0)),
            pl.BlockSpec((tm, d), lambda i: (i, 1)),
            pl.BlockSpec((tm, d), lambda i: (i, 0)),
            _resident(wa.shape), _resident(wb.shape), _resident(wo.shape), _resident(g.shape),
        ],
        out_specs=[pl.BlockSpec((tm, d), lambda i: (i, 0)), pl.BlockSpec((tm, d), lambda i: (i, 0))],
        out_shape=[jax.ShapeDtypeStruct((m, d), F32), jax.ShapeDtypeStruct((m, d), BF16)],
        compiler_params=_cparams(1),
        name="mix_out",
    )(a, b, sg, sg, x2, wa, wb, wo, g)


def _ffn_up_kernel(h_ref, wg_ref, wu_ref, cw_ref, cb_ref, o_ref, wgb_ref, wub_ref, tail_ref,
                   *, blocks_per_seq):
    i = pl.program_id(1)

    @pl.when(i == 0)
    def _():
        wgb_ref[...] = wg_ref[...].astype(BF16)
        wub_ref[...] = wu_ref[...].astype(BF16)

    @pl.when(i % blocks_per_seq == 0)
    def _():
        tail_ref[...] = jnp.zeros_like(tail_ref)

    h = h_ref[...]
    gate = _dot(h, wgb_ref[...])
    up = _dot(h, wub_ref[...])
    tm = gate.shape[0]
    ext = jnp.concatenate([tail_ref[...], gate], axis=0)
    tail_ref[...] = gate[tm - SUBLANES:, :]
    g_t1 = pltpu.roll(ext, 1, 0)[SUBLANES:, :]
    g_t2 = pltpu.roll(ext, 2, 0)[SUBLANES:, :]
    a = g_t2 * cw_ref[0:1, :]
    a = a + g_t1 * cw_ref[1:2, :]
    a = a + gate * cw_ref[2:3, :]
    a = a + cb_ref[...]
    o_ref[...] = (jax.nn.gelu(a) * up).astype(o_ref.dtype)


def _ffn_up(h, wg, wu, layer, cw, cb, seq, tm=1024, tf=512):
    m, d = h.shape
    f = wg.shape[2]
    kern = functools.partial(_ffn_up_kernel, blocks_per_seq=seq // tm)
    return pl.pallas_call(
        kern,
        grid=(f // tf, m // tm),
        in_specs=[
            pl.BlockSpec((tm, d), lambda j, i: (i, 0)),
            pl.BlockSpec((None, d, tf), lambda j, i: (layer, 0, j)),
            pl.BlockSpec((None, d, tf), lambda j, i: (layer, 0, j)),
            pl.BlockSpec((CONV_W, tf), lambda j, i: (0, j)),
            pl.BlockSpec((1, tf), lambda j, i: (0, j)),
        ],
        out_specs=pl.BlockSpec((tm, tf), lambda j, i: (i, j)),
        out_shape=jax.ShapeDtypeStruct((m, f), BF16),
        scratch_shapes=[pltpu.VMEM((d, tf), BF16), pltpu.VMEM((d, tf), BF16),
                        pltpu.VMEM((SUBLANES, tf), F32)],
        compiler_params=_cparams(2),
        name="ffn_up",
    )(h, wg, wu, cw, cb)


def _ffn_down_kernel(a_ref, x_ref, w_ref, o_ref):
    o_ref[...] = x_ref[...] + _dot(a_ref[...], w_ref[...])


def _ffn_down(act, x2, w, tm=512):
    m, d = x2.shape
    return pl.pallas_call(
        _ffn_down_kernel,
        grid=(m // tm,),
        in_specs=[
            pl.BlockSpec((tm, act.shape[1]), lambda i: (i, 0)),
            pl.BlockSpec((tm, d), lambda i: (i, 0)),
            _resident(w.shape),
        ],
        out_specs=pl.BlockSpec((tm, d), lambda i: (i, 0)),
        out_shape=jax.ShapeDtypeStruct((m, d), F32),
        compiler_params=_cparams(1),
        name="ffn_down",
    )(act, x2, w)


def _ple_kernel(x_ref, p_ref, g_ref, wg_ref, wp_ref, gn_ref, *out_refs):
    x = x_ref[...]
    h = _rms_rows(x, g_ref[...]).astype(BF16)
    gate = jax.nn.sigmoid(_dot(h, wg_ref[...]))
    proj = _dot(p_ref[...].astype(BF16), wp_ref[...])
    xn = x + proj * gate
    if len(out_refs) == 2:
        out_refs[0][...] = xn
    out_refs[-1][...] = _rms_rows(xn, gn_ref[...]).astype(out_refs[-1].dtype)


def _ple(x2, p2, g, wg, wp, g_next, last, tm=512):
    m, d = x2.shape
    row_spec = pl.BlockSpec((tm, d), lambda i: (i, 0))
    if last:
        out_specs, out_shape = [row_spec], [jax.ShapeDtypeStruct((m, d), F32)]
    else:
        out_specs = [row_spec, row_spec]
        out_shape = [jax.ShapeDtypeStruct((m, d), F32), jax.ShapeDtypeStruct((m, d), BF16)]
    return pl.pallas_call(
        _ple_kernel,
        grid=(m // tm,),
        in_specs=[
            row_spec,
            pl.BlockSpec((tm, PLE_DIM), lambda i: (i, 0)),
            _resident(g.shape), _resident(wg.shape), _resident(wp.shape), _resident(g_next.shape),
        ],
        out_specs=out_specs,
        out_shape=out_shape,
        compiler_params=_cparams(1),
        name="ple",
    )(x2, p2, g, wg, wp, g_next)


def _rotary_tables(seq):
    half = RET_DK // 2
    inv = ROPE_THETA ** (-jnp.arange(half, dtype=F32) / half)
    ang = jnp.arange(seq, dtype=jnp.int32).astype(F32)[:, None] * inv[None, :]
    cos = jnp.cos(ang)
    sin = jnp.sin(ang)
    return jnp.concatenate([cos, cos], axis=-1), jnp.concatenate([-sin, sin], axis=-1)


def _retention_tables():
    log_g = jnp.log1p(-jnp.exp2(-5.0 - jnp.arange(RET_HEADS, dtype=F32)))
    idx = jnp.arange(CHUNK, dtype=F32)
    diff = idx[:, None] - idx[None, :]
    causal = diff >= 0
    decay = jnp.where(causal[None], jnp.exp(log_g[:, None, None] * jnp.where(causal, diff, 0.0)[None]), 0.0)
    xi = jnp.exp(log_g[:, None] * (idx[None, :] + 1.0)).T
    zeta = jnp.exp(log_g[:, None] * (CHUNK - 1.0 - idx[None, :])).T
    chunk_decay = jnp.exp(log_g * CHUNK)
    xi_map = jnp.repeat(xi, RET_DV, axis=1)
    zeta_map = jnp.repeat(zeta, RET_DK, axis=1)
    cd = jnp.broadcast_to(chunk_decay[:, None, None], (RET_HEADS, 1, RET_DV))
    return decay, xi_map, zeta_map, cd


def kernel(x, p, mix_norm_g, w_in, sgu_norm_g, sgu_w, sgu_b, ret_norm_g, w_branch_a, w_branch_b, w_out, ffn_norm_g, ffn_w_gate, ffn_w_up, ffn_conv_w, ffn_conv_b, ffn_w_down, ple_norm_g, ple_w_gate, ple_w_proj, final_norm_g):
    batch, seq, d = x.shape
    depth = w_in.shape[0]
    m = batch * seq
    assert d == D_MODEL and seq % 1024 == 0 and w_in.shape[2] == IN_COLS

    cos_t, sin_t = _rotary_tables(seq)
    decay, xi_map, zeta_map, cd = _retention_tables()
    qk_scale = jnp.concatenate([jnp.ones((1, RET_QK_WIDTH), F32),
                                jnp.full((1, RET_QK_WIDTH), RET_DK ** -0.5, F32)], axis=1)
    rot = (cos_t, sin_t, qk_scale, seq)

    x2 = x.reshape(m, d)
    h = _norm_bf16(x2, mix_norm_g[0][None])
    for i in range(depth):
        z, wa, wb, wo, wpg = _in_proj(h, w_in, i, (w_branch_a, w_branch_b, w_out, ple_w_gate), *rot)
        sgu_bias = jnp.repeat(sgu_b[i].T, SGU_GROUP_DIM, axis=1)
        sg, wd, a, b = _gate_mixer(h, w_in, i, COL_G, 2 * D_MODEL, ffn_w_down, z,
                                   sgu_norm_g[i][None], sgu_w[i], sgu_bias, ret_norm_g[i][None],
                                   decay, xi_map, zeta_map, cd, seq)
        x2, hf = _mix_out(a, b, sg, x2, wa, wb, wo, ffn_norm_g[i][None])
        act = _ffn_up(hf, ffn_w_gate, ffn_w_up, i, ffn_conv_w[i], ffn_conv_b[i][None], seq)
        x2 = _ffn_down(act, x2, wd)
        last = i == depth - 1
        g_next = final_norm_g[None] if last else mix_norm_g[i + 1][None]
        outs = _ple(x2, p[i].reshape(m, PLE_DIM), ple_norm_g[i][None], wpg, ple_w_proj[i].astype(BF16),
                    g_next, last)
        if last:
            (out,) = outs
        else:
            x2, h = outs
    return out.reshape(batch, seq, d)
```

```python
import functools

import jax
import jax.numpy as jnp
from jax import lax
from jax.experimental import pallas as pl
from jax.experimental.pallas import tpu as pltpu

D_MODEL = 2048
CHUNK = 128
SGU_GROUPS = 8
SGU_GROUP_DIM = 128
SGU_WIDTH = SGU_GROUPS * SGU_GROUP_DIM
RET_HEADS = 8
RET_DK = 128
RET_DV = 256
RET_QK_WIDTH = RET_HEADS * RET_DK
RET_V_WIDTH = RET_HEADS * RET_DV
ROPE_THETA = 10000.0
D_FF = 5632
CONV_W = 3
PLE_DIM = 256
NORM_EPS = 1e-6
IN_COLS = 2 * SGU_WIDTH + 2 * RET_QK_WIDTH + 2 * RET_V_WIDTH + 2 * D_MODEL

V7X_VMEM_LIMIT_BYTES = 62 * 1024 * 1024
SUBLANES = 8
BF16_ROWS = 16

BF16 = jnp.bfloat16
F32 = jnp.float32

COL_UV = 0
COL_QK = 2 * SGU_WIDTH
COL_RV = COL_QK + 2 * RET_QK_WIDTH
COL_RG = COL_RV + RET_V_WIDTH
COL_G = COL_RG + RET_V_WIDTH


def _cparams(n_axes):
    return pltpu.CompilerParams(
        dimension_semantics=("arbitrary",) * n_axes,
        vmem_limit_bytes=V7X_VMEM_LIMIT_BYTES,
    )


def _rms_rows(x, g):
    y = x * lax.rsqrt(jnp.mean(x * x, axis=-1, keepdims=True) + NORM_EPS)
    return y * g


def _dot(a, b):
    return jnp.dot(a, b, preferred_element_type=F32)


def _norm_kernel(x_ref, g_ref, o_ref):
    o_ref[...] = _rms_rows(x_ref[...], g_ref[...]).astype(o_ref.dtype)


def _norm_bf16(x2, g, tm=1024):
    m, d = x2.shape
    return pl.pallas_call(
        _norm_kernel,
        grid=(m // tm,),
        in_specs=[pl.BlockSpec((tm, d), lambda i: (i, 0)), pl.BlockSpec((1, d), lambda i: (0, 0))],
        out_specs=pl.BlockSpec((tm, d), lambda i: (i, 0)),
        out_shape=jax.ShapeDtypeStruct((m, d), BF16),
        compiler_params=_cparams(1),
        name="norm",
    )(x2, g)


PROJ_SUB_ROWS = 256


def _cast_weight_once(w_ref, wb_ref):
    @pl.when(pl.program_id(1) == 0)
    def _():
        wb_ref[...] = w_ref[...].astype(BF16)


def _cast_sides(side_refs):
    n = len(side_refs) // 2
    for src, dst in zip(side_refs[:n], side_refs[n:]):
        dst[...] = src[...].astype(dst.dtype)


def _side_specs(sides, layer, n_steps, inner):
    in_specs, out_specs, out_shapes = [], [], []
    for w in sides:
        _, rows, cols = w.shape
        slab = rows // n_steps
        assert slab * n_steps == rows and slab % BF16_ROWS == 0
        in_specs.append(pl.BlockSpec((None, slab, cols), lambda j, i: (layer, j * inner + i, 0)))
        out_specs.append(pl.BlockSpec((slab, cols), lambda j, i: (j * inner + i, 0)))
        out_shapes.append(jax.ShapeDtypeStruct((rows, cols), BF16))
    return in_specs, out_specs, out_shapes


PROJ_TN = 1024
PROJ_KIND_BLOCKS = 2
N_PROJ_SIDES = 4


def _proj_kernel(h_ref, w_ref, cos_ref, sin_ref, scale_ref, *rest):
    n = N_PROJ_SIDES
    side_in, o_ref, side_out, wb_ref = rest[:n], rest[n], rest[n + 1:2 * n + 1], rest[2 * n + 1]
    kind = pl.program_id(0) // PROJ_KIND_BLOCKS
    _cast_weight_once(w_ref, wb_ref)
    _cast_sides(side_in + side_out)

    def slabs(sub, act):
        for r in range(h_ref.shape[0] // sub):
            rows = slice(r * sub, (r + 1) * sub)
            act(rows, _dot(h_ref[rows, :], wb_ref[...]))

    def store(fn):
        def act(rows, z):
            o_ref[rows, :] = fn(z).astype(o_ref.dtype)
        return act

    def rotary(rows, z):
        cos = cos_ref[rows, :]
        sin = sin_ref[rows, :]
        for h in range(z.shape[1] // RET_DK):
            sl = slice(h * RET_DK, (h + 1) * RET_DK)
            zh = z[:, sl]
            rot = zh * cos + pltpu.roll(zh, RET_DK // 2, 1) * sin
            o_ref[rows, sl] = (rot * scale_ref[:, sl]).astype(o_ref.dtype)

    @pl.when(kind == 0)
    def _():
        slabs(4 * PROJ_SUB_ROWS, store(jax.nn.gelu))

    @pl.when(kind == 1)
    def _():
        slabs(PROJ_SUB_ROWS, rotary)

    @pl.when(kind == 2)
    def _():
        slabs(4 * PROJ_SUB_ROWS, store(lambda z: z))

    @pl.when(kind == 3)
    def _():
        slabs(PROJ_SUB_ROWS, store(jax.nn.silu))


def _in_proj(h, w, layer, sides, cos_t, sin_t, scale, seq, tm=2048):
    m, d = h.shape
    tn = PROJ_TN
    grid = (4 * PROJ_KIND_BLOCKS, m // tm)
    n = grid[0] * tn
    assert len(sides) == N_PROJ_SIDES
    s_in, s_out, s_shapes = _side_specs(sides, layer, grid[0] * grid[1], grid[1])
    pos_blocks = seq // tm
    rot_block = lambda j: jnp.clip(j - PROJ_KIND_BLOCKS, 0, PROJ_KIND_BLOCKS - 1)
    return pl.pallas_call(
        _proj_kernel,
        grid=grid,
        in_specs=[
            pl.BlockSpec((tm, d), lambda j, i: (i, 0)),
            pl.BlockSpec((None, d, tn), lambda j, i: (layer, 0, j)),
            pl.BlockSpec((tm, RET_DK), lambda j, i: (i % pos_blocks, 0)),
            pl.BlockSpec((tm, RET_DK), lambda j, i: (i % pos_blocks, 0)),
            pl.BlockSpec((1, tn), lambda j, i: (0, rot_block(j))),
        ] + s_in,
        out_specs=[pl.BlockSpec((tm, tn), lambda j, i: (i, j))] + s_out,
        out_shape=[jax.ShapeDtypeStruct((m, n), BF16)] + s_shapes,
        scratch_shapes=[pltpu.VMEM((d, tn), BF16)],
        compiler_params=_cparams(2),
        name="in_proj",
    )(h, w, cos_t, sin_t, scale, *sides)


MIXER_CHUNKS_PER_STEP = 2


def _sgu_pair(gu_ref, gv_ref, sgu_g_ref, sgu_w_ref, sgu_bias_ref, a_ref):
    row = lax.broadcasted_iota(jnp.int32, (CHUNK, CHUNK), 0)
    col = lax.broadcasted_iota(jnp.int32, (CHUNK, CHUNK), 1)
    causal = row >= col
    chunks = [slice(c * CHUNK, (c + 1) * CHUNK) for c in range(MIXER_CHUNKS_PER_STEP)]
    vn = [_rms_rows(gv_ref[rows, :].astype(F32), sgu_g_ref[...]).astype(BF16) for rows in chunks]
    for g in range(SGU_GROUPS):
        sl = slice(g * SGU_GROUP_DIM, (g + 1) * SGU_GROUP_DIM)
        w = jnp.where(causal, sgu_w_ref[g], 0.0).astype(BF16)
        s = _dot(w, jnp.concatenate([v[:, sl] for v in vn], axis=1))
        for c, rows in enumerate(chunks):
            s_c = s[:, c * SGU_GROUP_DIM:(c + 1) * SGU_GROUP_DIM] + sgu_bias_ref[:, sl]
            a_ref[rows, sl] = (gu_ref[rows, sl].astype(F32) * s_c).astype(a_ref.dtype)


def _retention_scores(c, q_ref, k_ref, rv_ref, xi_ref, zeta_ref, state_ref):
    rows = slice(c * CHUNK, (c + 1) * CHUNK)
    out = []
    for h in range(RET_HEADS):
        ksl = slice(h * RET_DK, (h + 1) * RET_DK)
        vsl = slice(h * RET_DV, (h + 1) * RET_DV)
        q = q_ref[rows, ksl]
        k = k_ref[rows, ksl]
        v = rv_ref[rows, vsl]
        scores = lax.dot_general(q, k, (((1,), (1,)), ((), ())), preferred_element_type=F32)
        kz = (k.astype(F32) * zeta_ref[:, ksl]).astype(BF16)
        prev = state_ref[h]
        q_xi = (q.astype(F32) * xi_ref[:, h * RET_DV:h * RET_DV + RET_DK]).astype(BF16)
        out.append((scores, kz, prev, q_xi, v))
    return out


def _retention_outputs(c, stage1, srg_ref, ret_g_ref, decay_ref, cd_ref, b_ref, state_ref):
    rows = slice(c * CHUNK, (c + 1) * CHUNK)
    for h, (scores, kz, prev, q_xi, v) in enumerate(stage1):
        vsl = slice(h * RET_DV, (h + 1) * RET_DV)
        lhs = jnp.concatenate([(scores * decay_ref[h]).astype(BF16), q_xi], axis=1)
        rhs = jnp.concatenate([v, prev.astype(BF16)], axis=0)
        y = _dot(lhs, rhs)
        kv = lax.dot_general(kz, v, (((0,), (0,)), ((), ())), preferred_element_type=F32)
        state_ref[h] = prev * cd_ref[h] + kv
        y = _rms_rows(y, ret_g_ref[:, vsl])
        b_ref[rows, vsl] = (srg_ref[rows, vsl].astype(F32) * y).astype(b_ref.dtype)


def _gate_mixer_kernel(h_ref, w_ref, wd_ref, gu_ref, gv_ref, q_ref, k_ref, rv_ref, srg_ref,
                       sgu_g_ref, sgu_w_ref, sgu_bias_ref, ret_g_ref, decay_ref, xi_ref, zeta_ref, cd_ref,
                       o_ref, wdo_ref, a_ref, b_ref, wb_ref, state_ref, *, steps_per_seq, sub):
    step = pl.program_id(0) * pl.num_programs(1) + pl.program_id(1)
    _cast_weight_once(w_ref, wb_ref)

    @pl.when(step % steps_per_seq == 0)
    def _():
        state_ref[...] = jnp.zeros_like(state_ref)

    wdo_ref[...] = wd_ref[...].astype(wdo_ref.dtype)

    def gate_slab(r):
        rows = slice(r * sub, (r + 1) * sub)
        o_ref[rows, :] = jax.nn.sigmoid(_dot(h_ref[rows, :], wb_ref[...])).astype(o_ref.dtype)

    assert h_ref.shape[0] // sub == 4 and MIXER_CHUNKS_PER_STEP == 2
    stage1 = _retention_scores(0, q_ref, k_ref, rv_ref, xi_ref, zeta_ref, state_ref)
    gate_slab(0)
    _sgu_pair(gu_ref, gv_ref, sgu_g_ref, sgu_w_ref, sgu_bias_ref, a_ref)
    _retention_outputs(0, stage1, srg_ref, ret_g_ref, decay_ref, cd_ref, b_ref, state_ref)
    gate_slab(1)
    stage1 = _retention_scores(1, q_ref, k_ref, rv_ref, xi_ref, zeta_ref, state_ref)
    gate_slab(2)
    _retention_outputs(1, stage1, srg_ref, ret_g_ref, decay_ref, cd_ref, b_ref, state_ref)
    gate_slab(3)


def _gate_mixer(h, w, layer, col0, n, w_down, z, sgu_g, sgu_w, sgu_bias, ret_g,
                decay, xi_map, zeta_map, cd, seq, tm=1024, tn=1024, sub=PROJ_SUB_ROWS):
    m, d = h.shape
    off = col0 // tn
    grid = (n // tn, m // tm)
    inner = grid[1]
    mix_rows = MIXER_CHUNKS_PER_STEP * CHUNK
    assert grid[0] * grid[1] * mix_rows == m and seq % mix_rows == 0
    (wd_in,), (wd_out,), (wd_shape,) = _side_specs((w_down,), layer, grid[0] * grid[1], inner)

    def rows(width, blk):
        return pl.BlockSpec((mix_rows, width), lambda j, i: (j * inner + i, blk))

    def const(a):
        return pl.BlockSpec(a.shape, lambda j, i: (0,) * a.ndim, pipeline_mode=pl.Buffered(1))

    consts = (sgu_g, sgu_w, sgu_bias, ret_g, decay, xi_map, zeta_map, cd)
    kern = functools.partial(_gate_mixer_kernel, steps_per_seq=seq // mix_rows, sub=sub)
    return pl.pallas_call(
        kern,
        grid=grid,
        in_specs=[
            pl.BlockSpec((tm, d), lambda j, i: (i, 0)),
            pl.BlockSpec((None, d, tn), lambda j, i: (layer, 0, off + j)),
            wd_in,
            rows(SGU_WIDTH, COL_UV // SGU_WIDTH), rows(SGU_WIDTH, COL_UV // SGU_WIDTH + 1),
            rows(RET_QK_WIDTH, COL_QK // RET_QK_WIDTH), rows(RET_QK_WIDTH, COL_QK // RET_QK_WIDTH + 1),
            rows(RET_V_WIDTH, COL_RV // RET_V_WIDTH), rows(RET_V_WIDTH, COL_RG // RET_V_WIDTH),
        ] + [const(c) for c in consts],
        out_specs=[pl.BlockSpec((tm, tn), lambda j, i: (i, j)), wd_out,
                   rows(SGU_WIDTH, 0), rows(RET_V_WIDTH, 0)],
        out_shape=[jax.ShapeDtypeStruct((m, n), BF16), wd_shape,
                   jax.ShapeDtypeStruct((m, SGU_WIDTH), BF16),
                   jax.ShapeDtypeStruct((m, RET_V_WIDTH), BF16)],
        scratch_shapes=[pltpu.VMEM((d, tn), BF16), pltpu.VMEM((RET_HEADS, RET_DK, RET_DV), F32)],
        compiler_params=_cparams(2),
        name="gate_mixer",
    )(h, w, w_down, z, z, z, z, z, z, *consts)


def _resident(shape):
    return pl.BlockSpec(shape, lambda *_: (0,) * len(shape), pipeline_mode=pl.Buffered(1))


def _mix_out_kernel(a_ref, b_ref, sga_ref, sgb_ref, x_ref, wa_ref, wb_ref, wo_ref, g_ref, xo_ref, h_ref):
    ya = _dot(a_ref[...], wa_ref[...])
    yb = _dot(b_ref[...], wb_ref[...])
    m = (sga_ref[...].astype(F32) * ya + sgb_ref[...].astype(F32) * yb).astype(BF16)
    xn = x_ref[...] + _dot(m, wo_ref[...])
    xo_ref[...] = xn
    h_ref[...] = _rms_rows(xn, g_ref[...]).astype(h_ref.dtype)


def _mix_out(a, b, sg, x2, wa, wb, wo, g, tm=512):
    m, d = x2.shape
    return pl.pallas_call(
        _mix_out_kernel,
        grid=(m // tm,),
        in_specs=[
            pl.BlockSpec((tm, a.shape[1]), lambda i: (i, 0)),
            pl.BlockSpec((tm, b.shape[1]), lambda i: (i, 0)),
            pl.BlockSpec((tm, d), lambda i: (i, 0)),
            pl.BlockSpec((tm, d), lambda i: (i, 1)),
            pl.BlockSpec((tm, d), lambda i: (i, 0)),
            _resident(wa.shape), _resident(wb.shape), _resident(wo.shape), _resident(g.shape),
        ],
        out_specs=[pl.BlockSpec((tm, d), lambda i: (i, 0)), pl.BlockSpec((tm, d), lambda i: (i, 0))],
        out_shape=[jax.ShapeDtypeStruct((m, d), F32), jax.ShapeDtypeStruct((m, d), BF16)],
        compiler_params=_cparams(1),
        name="mix_out",
    )(a, b, sg, sg, x2, wa, wb, wo, g)


FFN_SUB_ROWS = 1024


def _ffn_up_kernel(h_ref, wg_ref, wu_ref, cw_ref, cb_ref, o_ref, wgb_ref, wub_ref, tail_ref,
                   *, blocks_per_seq):
    i = pl.program_id(1)

    @pl.when(i == 0)
    def _():
        wgb_ref[...] = wg_ref[...].astype(BF16)
        wub_ref[...] = wu_ref[...].astype(BF16)

    @pl.when(i % blocks_per_seq == 0)
    def _():
        tail_ref[...] = jnp.zeros_like(tail_ref)

    tail = tail_ref[...]
    for r in range(h_ref.shape[0] // FFN_SUB_ROWS):
        rows = slice(r * FFN_SUB_ROWS, (r + 1) * FFN_SUB_ROWS)
        h = h_ref[rows, :]
        gate = _dot(h, wgb_ref[...])
        up = _dot(h, wub_ref[...])
        ext = jnp.concatenate([tail, gate], axis=0)
        tail = gate[FFN_SUB_ROWS - SUBLANES:, :]
        g_t1 = pltpu.roll(ext, 1, 0)[SUBLANES:, :]
        g_t2 = pltpu.roll(ext, 2, 0)[SUBLANES:, :]
        a = g_t2 * cw_ref[0:1, :]
        a = a + g_t1 * cw_ref[1:2, :]
        a = a + gate * cw_ref[2:3, :]
        a = a + cb_ref[...]
        o_ref[rows, :] = (jax.nn.gelu(a) * up).astype(o_ref.dtype)
    tail_ref[...] = tail


def _ffn_up(h, wg, wu, layer, cw, cb, seq, tm=2048, tf=512):
    m, d = h.shape
    f = wg.shape[2]
    kern = functools.partial(_ffn_up_kernel, blocks_per_seq=seq // tm)
    return pl.pallas_call(
        kern,
        grid=(f // tf, m // tm),
        in_specs=[
            pl.BlockSpec((tm, d), lambda j, i: (i, 0)),
            pl.BlockSpec((None, d, tf), lambda j, i: (layer, 0, j)),
            pl.BlockSpec((None, d, tf), lambda j, i: (layer, 0, j)),
            pl.BlockSpec((CONV_W, tf), lambda j, i: (0, j)),
            pl.BlockSpec((1, tf), lambda j, i: (0, j)),
        ],
        out_specs=pl.BlockSpec((tm, tf), lambda j, i: (i, j)),
        out_shape=jax.ShapeDtypeStruct((m, f), BF16),
        scratch_shapes=[pltpu.VMEM((d, tf), BF16), pltpu.VMEM((d, tf), BF16),
                        pltpu.VMEM((SUBLANES, tf), F32)],
        compiler_params=_cparams(2),
        name="ffn_up",
    )(h, wg, wu, cw, cb)


def _ffn_down_kernel(a_ref, x_ref, w_ref, o_ref):
    o_ref[...] = x_ref[...] + _dot(a_ref[...], w_ref[...])


def _ffn_down(act, x2, w, tm=512):
    m, d = x2.shape
    return pl.pallas_call(
        _ffn_down_kernel,
        grid=(m // tm,),
        in_specs=[
            pl.BlockSpec((tm, act.shape[1]), lambda i: (i, 0)),
            pl.BlockSpec((tm, d), lambda i: (i, 0)),
            _resident(w.shape),
        ],
        out_specs=pl.BlockSpec((tm, d), lambda i: (i, 0)),
        out_shape=jax.ShapeDtypeStruct((m, d), F32),
        compiler_params=_cparams(1),
        name="ffn_down",
    )(act, x2, w)


def _ple_kernel(x_ref, p_ref, g_ref, wg_ref, wp_ref, gn_ref, *out_refs):
    x = x_ref[...]
    h = _rms_rows(x, g_ref[...]).astype(BF16)
    gate = jax.nn.sigmoid(_dot(h, wg_ref[...]))
    proj = _dot(p_ref[...].astype(BF16), wp_ref[...])
    xn = x + proj * gate
    if len(out_refs) == 2:
        out_refs[0][...] = xn
    out_refs[-1][...] = _rms_rows(xn, gn_ref[...]).astype(out_refs[-1].dtype)


def _ple(x2, p2, g, wg, wp, g_next, last, tm=512):
    m, d = x2.shape
    row_spec = pl.BlockSpec((tm, d), lambda i: (i, 0))
    if last:
        out_specs, out_shape = [row_spec], [jax.ShapeDtypeStruct((m, d), F32)]
    else:
        out_specs = [row_spec, row_spec]
        out_shape = [jax.ShapeDtypeStruct((m, d), F32), jax.ShapeDtypeStruct((m, d), BF16)]
    return pl.pallas_call(
        _ple_kernel,
        grid=(m // tm,),
        in_specs=[
            row_spec,
            pl.BlockSpec((tm, PLE_DIM), lambda i: (i, 0)),
            _resident(g.shape), _resident(wg.shape), _resident(wp.shape), _resident(g_next.shape),
        ],
        out_specs=out_specs,
        out_shape=out_shape,
        compiler_params=_cparams(1),
        name="ple",
    )(x2, p2, g, wg, wp, g_next)


def _rotary_tables(seq):
    half = RET_DK // 2
    inv = ROPE_THETA ** (-jnp.arange(half, dtype=F32) / half)
    lo = jnp.arange(CHUNK, dtype=jnp.int32).astype(F32)[:, None] * inv[None, :]
    hi = (jnp.arange(seq // CHUNK, dtype=jnp.int32) * CHUNK).astype(F32)[:, None] * inv[None, :]
    cl, sl = jnp.cos(lo)[None], jnp.sin(lo)[None]
    ch, sh = jnp.cos(hi)[:, None, :], jnp.sin(hi)[:, None, :]
    cos = (ch * cl - sh * sl).reshape(seq, half)
    sin = (sh * cl + ch * sl).reshape(seq, half)
    return jnp.concatenate([cos, cos], axis=-1), jnp.concatenate([-sin, sin], axis=-1)


def _retention_tables():
    log_g = jnp.log1p(-jnp.exp2(-5.0 - jnp.arange(RET_HEADS, dtype=F32)))
    idx = jnp.arange(CHUNK, dtype=F32)
    diff = idx[:, None] - idx[None, :]
    causal = diff >= 0
    decay = jnp.where(causal[None], jnp.exp(log_g[:, None, None] * jnp.where(causal, diff, 0.0)[None]), 0.0)
    xi = jnp.exp(log_g[:, None] * (idx[None, :] + 1.0)).T
    zeta = jnp.exp(log_g[:, None] * (CHUNK - 1.0 - idx[None, :])).T
    chunk_decay = jnp.exp(log_g * CHUNK)
    xi_map = jnp.repeat(xi, RET_DV, axis=1)
    zeta_map = jnp.repeat(zeta, RET_DK, axis=1)
    cd = jnp.broadcast_to(chunk_decay[:, None, None], (RET_HEADS, 1, RET_DV))
    return decay, xi_map, zeta_map, cd


def kernel(x, p, mix_norm_g, w_in, sgu_norm_g, sgu_w, sgu_b, ret_norm_g, w_branch_a, w_branch_b, w_out, ffn_norm_g, ffn_w_gate, ffn_w_up, ffn_conv_w, ffn_conv_b, ffn_w_down, ple_norm_g, ple_w_gate, ple_w_proj, final_norm_g):
    batch, seq, d = x.shape
    depth = w_in.shape[0]
    m = batch * seq
    assert d == D_MODEL and seq % 1024 == 0 and w_in.shape[2] == IN_COLS

    cos_t, sin_t = _rotary_tables(seq)
    decay, xi_map, zeta_map, cd = _retention_tables()
    qk_scale = jnp.concatenate([jnp.ones((1, RET_QK_WIDTH), F32),
                                jnp.full((1, RET_QK_WIDTH), RET_DK ** -0.5, F32)], axis=1)
    rot = (cos_t, sin_t, qk_scale, seq)

    x2 = x.reshape(m, d)
    h = _norm_bf16(x2, mix_norm_g[0][None])
    for i in range(depth):
        z, wa, wb, wo, wpg = _in_proj(h, w_in, i, (w_branch_a, w_branch_b, w_out, ple_w_gate), *rot)
        sgu_bias = jnp.repeat(sgu_b[i].T, SGU_GROUP_DIM, axis=1)
        sg, wd, a, b = _gate_mixer(h, w_in, i, COL_G, 2 * D_MODEL, ffn_w_down, z,
                                   sgu_norm_g[i][None], sgu_w[i], sgu_bias, ret_norm_g[i][None],
                                   decay, xi_map, zeta_map, cd, seq)
        x2, hf = _mix_out(a, b, sg, x2, wa, wb, wo, ffn_norm_g[i][None])
        act = _ffn_up(hf, ffn_w_gate, ffn_w_up, i, ffn_conv_w[i], ffn_conv_b[i][None], seq)
        x2 = _ffn_down(act, x2, wd)
        last = i == depth - 1
        g_next = final_norm_g[None] if last else mix_norm_g[i + 1][None]
        outs = _ple(x2, p[i].reshape(m, PLE_DIM), ple_norm_g[i][None], wpg, ple_w_proj[i].astype(BF16),
                    g_next, last)
        if last:
            (out,) = outs
        else:
            x2, h = outs
    return out.reshape(batch, seq, d)
```

```python
import functools

import jax
import jax.numpy as jnp
from jax import lax
from jax.experimental import pallas as pl
from jax.experimental.pallas import tpu as pltpu

D_MODEL = 2048
CHUNK = 128
SGU_GROUPS = 8
SGU_GROUP_DIM = 128
SGU_WIDTH = SGU_GROUPS * SGU_GROUP_DIM
RET_HEADS = 8
RET_DK = 128
RET_DV = 256
RET_QK_WIDTH = RET_HEADS * RET_DK
RET_V_WIDTH = RET_HEADS * RET_DV
ROPE_THETA = 10000.0
D_FF = 5632
CONV_W = 3
PLE_DIM = 256
NORM_EPS = 1e-6
IN_COLS = 2 * SGU_WIDTH + 2 * RET_QK_WIDTH + 2 * RET_V_WIDTH + 2 * D_MODEL

V7X_VMEM_LIMIT_BYTES = 62 * 1024 * 1024
SUBLANES = 8
BF16_ROWS = 16

BF16 = jnp.bfloat16
F32 = jnp.float32

COL_UV = 0
COL_QK = 2 * SGU_WIDTH
COL_RV = COL_QK + 2 * RET_QK_WIDTH
COL_RG = COL_RV + RET_V_WIDTH
COL_G = COL_RG + RET_V_WIDTH


def _cparams(n_axes):
    return pltpu.CompilerParams(
        dimension_semantics=("arbitrary",) * n_axes,
        vmem_limit_bytes=V7X_VMEM_LIMIT_BYTES,
    )


def _rms_rows(x, g):
    y = x * lax.rsqrt(jnp.mean(x * x, axis=-1, keepdims=True) + NORM_EPS)
    return y * g


def _dot(a, b):
    return jnp.dot(a, b, preferred_element_type=F32)


def _norm_kernel(x_ref, g_ref, o_ref):
    o_ref[...] = _rms_rows(x_ref[...], g_ref[...]).astype(o_ref.dtype)


def _norm_bf16(x2, g, tm=1024):
    m, d = x2.shape
    return pl.pallas_call(
        _norm_kernel,
        grid=(m // tm,),
        in_specs=[pl.BlockSpec((tm, d), lambda i: (i, 0)), pl.BlockSpec((1, d), lambda i: (0, 0))],
        out_specs=pl.BlockSpec((tm, d), lambda i: (i, 0)),
        out_shape=jax.ShapeDtypeStruct((m, d), BF16),
        compiler_params=_cparams(1),
        name="norm",
    )(x2, g)


PROJ_SUB_ROWS = 256


def _cast_weight_once(w_ref, wb_ref):
    @pl.when(pl.program_id(1) == 0)
    def _():
        wb_ref[...] = w_ref[...].astype(BF16)


def _cast_sides(side_refs):
    n = len(side_refs) // 2
    for src, dst in zip(side_refs[:n], side_refs[n:]):
        dst[...] = src[...].astype(dst.dtype)


def _side_specs(sides, layer, n_steps, inner):
    in_specs, out_specs, out_shapes = [], [], []
    for w in sides:
        _, rows, cols = w.shape
        slab = rows // n_steps
        assert slab * n_steps == rows and slab % BF16_ROWS == 0
        in_specs.append(pl.BlockSpec((None, slab, cols), lambda j, i: (layer, j * inner + i, 0)))
        out_specs.append(pl.BlockSpec((slab, cols), lambda j, i: (j * inner + i, 0)))
        out_shapes.append(jax.ShapeDtypeStruct((rows, cols), BF16))
    return in_specs, out_specs, out_shapes


PROJ_TN = 1024
IN_PROJ_SLAB_ROWS = 1024
PROJ_KIND_BLOCKS = 2
N_PROJ_SIDES = 4


def _proj_kernel(h_ref, w_ref, cos_ref, sin_ref, scale_ref, *rest):
    n = N_PROJ_SIDES
    side_in, o_ref, side_out, wb_ref = rest[:n], rest[n], rest[n + 1:2 * n + 1], rest[2 * n + 1]
    kind = pl.program_id(0) // PROJ_KIND_BLOCKS
    _cast_weight_once(w_ref, wb_ref)
    _cast_sides(side_in + side_out)

    def slabs(sub, act):
        for r in range(h_ref.shape[0] // sub):
            rows = slice(r * sub, (r + 1) * sub)
            act(rows, _dot(h_ref[rows, :], wb_ref[...]))

    def store(fn):
        def act(rows, z):
            o_ref[rows, :] = fn(z).astype(o_ref.dtype)
        return act

    def rotary(rows, z):
        cos = cos_ref[rows, :]
        sin = sin_ref[rows, :]
        for h in range(z.shape[1] // RET_DK):
            sl = slice(h * RET_DK, (h + 1) * RET_DK)
            zh = z[:, sl]
            rot = zh * cos + pltpu.roll(zh, RET_DK // 2, 1) * sin
            o_ref[rows, sl] = (rot * scale_ref[:, sl]).astype(o_ref.dtype)

    @pl.when(kind == 0)
    def _():
        slabs(IN_PROJ_SLAB_ROWS, store(jax.nn.gelu))

    @pl.when(kind == 1)
    def _():
        slabs(IN_PROJ_SLAB_ROWS, rotary)

    @pl.when(kind == 2)
    def _():
        slabs(IN_PROJ_SLAB_ROWS, store(lambda z: z))

    @pl.when(kind == 3)
    def _():
        slabs(IN_PROJ_SLAB_ROWS, store(jax.nn.silu))


def _in_proj(h, w, layer, sides, cos_t, sin_t, scale, seq, tm=2048):
    m, d = h.shape
    tn = PROJ_TN
    grid = (4 * PROJ_KIND_BLOCKS, m // tm)
    n = grid[0] * tn
    assert len(sides) == N_PROJ_SIDES
    s_in, s_out, s_shapes = _side_specs(sides, layer, grid[0] * grid[1], grid[1])
    pos_blocks = seq // tm
    rot_block = lambda j: jnp.clip(j - PROJ_KIND_BLOCKS, 0, PROJ_KIND_BLOCKS - 1)
    return pl.pallas_call(
        _proj_kernel,
        grid=grid,
        in_specs=[
            pl.BlockSpec((tm, d), lambda j, i: (i, 0)),
            pl.BlockSpec((None, d, tn), lambda j, i: (layer, 0, j)),
            pl.BlockSpec((tm, RET_DK), lambda j, i: (i % pos_blocks, 0)),
            pl.BlockSpec((tm, RET_DK), lambda j, i: (i % pos_blocks, 0)),
            pl.BlockSpec((1, tn), lambda j, i: (0, rot_block(j))),
        ] + s_in,
        out_specs=[pl.BlockSpec((tm, tn), lambda j, i: (i, j))] + s_out,
        out_shape=[jax.ShapeDtypeStruct((m, n), BF16)] + s_shapes,
        scratch_shapes=[pltpu.VMEM((d, tn), BF16)],
        compiler_params=_cparams(2),
        name="in_proj",
    )(h, w, cos_t, sin_t, scale, *sides)


MIXER_CHUNKS_PER_STEP = 2


def _sgu_pair(gu_ref, gv_ref, sgu_g_ref, sgu_w_ref, sgu_bias_ref, a_ref):
    row = lax.broadcasted_iota(jnp.int32, (CHUNK, CHUNK), 0)
    col = lax.broadcasted_iota(jnp.int32, (CHUNK, CHUNK), 1)
    causal = row >= col
    chunks = [slice(c * CHUNK, (c + 1) * CHUNK) for c in range(MIXER_CHUNKS_PER_STEP)]
    vn = [_rms_rows(gv_ref[rows, :].astype(F32), sgu_g_ref[...]).astype(BF16) for rows in chunks]
    for g in range(SGU_GROUPS):
        sl = slice(g * SGU_GROUP_DIM, (g + 1) * SGU_GROUP_DIM)
        w = jnp.where(causal, sgu_w_ref[g], 0.0).astype(BF16)
        s = _dot(w, jnp.concatenate([v[:, sl] for v in vn], axis=1))
        for c, rows in enumerate(chunks):
            s_c = s[:, c * SGU_GROUP_DIM:(c + 1) * SGU_GROUP_DIM] + sgu_bias_ref[:, sl]
            a_ref[rows, sl] = (gu_ref[rows, sl].astype(F32) * s_c).astype(a_ref.dtype)


def _retention_scores(c, q_ref, k_ref, rv_ref, xi_ref, zeta_ref, state_ref):
    rows = slice(c * CHUNK, (c + 1) * CHUNK)
    out = []
    for h in range(RET_HEADS):
        ksl = slice(h * RET_DK, (h + 1) * RET_DK)
        vsl = slice(h * RET_DV, (h + 1) * RET_DV)
        q = q_ref[rows, ksl]
        k = k_ref[rows, ksl]
        v = rv_ref[rows, vsl]
        scores = lax.dot_general(q, k, (((1,), (1,)), ((), ())), preferred_element_type=F32)
        kz = (k.astype(F32) * zeta_ref[:, ksl]).astype(BF16)
        prev = state_ref[h]
        q_xi = (q.astype(F32) * xi_ref[:, h * RET_DV:h * RET_DV + RET_DK]).astype(BF16)
        out.append((scores, kz, prev, q_xi, v))
    return out


def _retention_outputs(c, stage1, srg_ref, ret_g_ref, decay_ref, cd_ref, b_ref, state_ref):
    rows = slice(c * CHUNK, (c + 1) * CHUNK)
    for h, (scores, kz, prev, q_xi, v) in enumerate(stage1):
        vsl = slice(h * RET_DV, (h + 1) * RET_DV)
        lhs = jnp.concatenate([(scores * decay_ref[h]).astype(BF16), q_xi], axis=1)
        rhs = jnp.concatenate([v, prev.astype(BF16)], axis=0)
        y = _dot(lhs, rhs)
        kv = lax.dot_general(kz, v, (((0,), (0,)), ((), ())), preferred_element_type=F32)
        state_ref[h] = prev * cd_ref[h] + kv
        y = _rms_rows(y, ret_g_ref[:, vsl])
        b_ref[rows, vsl] = (srg_ref[rows, vsl].astype(F32) * y).astype(b_ref.dtype)


def _gate_mixer_kernel(h_ref, w_ref, wd_ref, gu_ref, gv_ref, q_ref, k_ref, rv_ref, srg_ref,
                       sgu_g_ref, sgu_w_ref, sgu_bias_ref, ret_g_ref, decay_ref, xi_ref, zeta_ref, cd_ref,
                       o_ref, wdo_ref, a_ref, b_ref, wb_ref, state_ref, *, steps_per_seq, sub):
    step = pl.program_id(0) * pl.num_programs(1) + pl.program_id(1)
    _cast_weight_once(w_ref, wb_ref)

    @pl.when(step % steps_per_seq == 0)
    def _():
        state_ref[...] = jnp.zeros_like(state_ref)

    wdo_ref[...] = wd_ref[...].astype(wdo_ref.dtype)

    def gate_slab(r):
        rows = slice(r * sub, (r + 1) * sub)
        o_ref[rows, :] = jax.nn.sigmoid(_dot(h_ref[rows, :], wb_ref[...])).astype(o_ref.dtype)

    assert h_ref.shape[0] // sub == 4 and MIXER_CHUNKS_PER_STEP == 2
    stage1 = _retention_scores(0, q_ref, k_ref, rv_ref, xi_ref, zeta_ref, state_ref)
    gate_slab(0)
    _sgu_pair(gu_ref, gv_ref, sgu_g_ref, sgu_w_ref, sgu_bias_ref, a_ref)
    _retention_outputs(0, stage1, srg_ref, ret_g_ref, decay_ref, cd_ref, b_ref, state_ref)
    gate_slab(1)
    stage1 = _retention_scores(1, q_ref, k_ref, rv_ref, xi_ref, zeta_ref, state_ref)
    gate_slab(2)
    _retention_outputs(1, stage1, srg_ref, ret_g_ref, decay_ref, cd_ref, b_ref, state_ref)
    gate_slab(3)


def _gate_mixer(h, w, layer, col0, n, w_down, z, sgu_g, sgu_w, sgu_bias, ret_g,
                decay, xi_map, zeta_map, cd, seq, tm=1024, tn=1024, sub=PROJ_SUB_ROWS):
    m, d = h.shape
    off = col0 // tn
    grid = (n // tn, m // tm)
    inner = grid[1]
    mix_rows = MIXER_CHUNKS_PER_STEP * CHUNK
    assert grid[0] * grid[1] * mix_rows == m and seq % mix_rows == 0
    (wd_in,), (wd_out,), (wd_shape,) = _side_specs((w_down,), layer, grid[0] * grid[1], inner)

    def rows(width, blk):
        return pl.BlockSpec((mix_rows, width), lambda j, i: (j * inner + i, blk))

    def const(a):
        return pl.BlockSpec(a.shape, lambda j, i: (0,) * a.ndim, pipeline_mode=pl.Buffered(1))

    consts = (sgu_g, sgu_w, sgu_bias, ret_g, decay, xi_map, zeta_map, cd)
    kern = functools.partial(_gate_mixer_kernel, steps_per_seq=seq // mix_rows, sub=sub)
    return pl.pallas_call(
        kern,
        grid=grid,
        in_specs=[
            pl.BlockSpec((tm, d), lambda j, i: (i, 0)),
            pl.BlockSpec((None, d, tn), lambda j, i: (layer, 0, off + j)),
            wd_in,
            rows(SGU_WIDTH, COL_UV // SGU_WIDTH), rows(SGU_WIDTH, COL_UV // SGU_WIDTH + 1),
            rows(RET_QK_WIDTH, COL_QK // RET_QK_WIDTH), rows(RET_QK_WIDTH, COL_QK // RET_QK_WIDTH + 1),
            rows(RET_V_WIDTH, COL_RV // RET_V_WIDTH), rows(RET_V_WIDTH, COL_RG // RET_V_WIDTH),
        ] + [const(c) for c in consts],
        out_specs=[pl.BlockSpec((tm, tn), lambda j, i: (i, j)), wd_out,
                   rows(SGU_WIDTH, 0), rows(RET_V_WIDTH, 0)],
        out_shape=[jax.ShapeDtypeStruct((m, n), BF16), wd_shape,
                   jax.ShapeDtypeStruct((m, SGU_WIDTH), BF16),
                   jax.ShapeDtypeStruct((m, RET_V_WIDTH), BF16)],
        scratch_shapes=[pltpu.VMEM((d, tn), BF16), pltpu.VMEM((RET_HEADS, RET_DK, RET_DV), F32)],
        compiler_params=_cparams(2),
        name="gate_mixer",
    )(h, w, w_down, z, z, z, z, z, z, *consts)


def _resident(shape):
    return pl.BlockSpec(shape, lambda *_: (0,) * len(shape), pipeline_mode=pl.Buffered(1))


def _mix_out_kernel(a_ref, b_ref, sga_ref, sgb_ref, x_ref, wa_ref, wb_ref, wo_ref, g_ref, xo_ref, h_ref):
    ya = _dot(a_ref[...], wa_ref[...])
    yb = _dot(b_ref[...], wb_ref[...])
    m = (sga_ref[...].astype(F32) * ya + sgb_ref[...].astype(F32) * yb).astype(BF16)
    xn = x_ref[...] + _dot(m, wo_ref[...])
    xo_ref[...] = xn
    h_ref[...] = _rms_rows(xn, g_ref[...]).astype(h_ref.dtype)


def _mix_out(a, b, sg, x2, wa, wb, wo, g, tm=512):
    m, d = x2.shape
    return pl.pallas_call(
        _mix_out_kernel,
        grid=(m // tm,),
        in_specs=[
            pl.BlockSpec((tm, a.shape[1]), lambda i: (i, 0)),
            pl.BlockSpec((tm, b.shape[1]), lambda i: (i, 0)),
            pl.BlockSpec((tm, d), lambda i: (i, 0)),
            pl.BlockSpec((tm, d), lambda i: (i, 1)),
            pl.BlockSpec((tm, d), lambda i: (i, 0)),
            _resident(wa.shape), _resident(wb.shape), _resident(wo.shape), _resident(g.shape),
        ],
        out_specs=[pl.BlockSpec((tm, d), lambda i: (i, 0)), pl.BlockSpec((tm, d), lambda i: (i, 0))],
        out_shape=[jax.ShapeDtypeStruct((m, d), F32), jax.ShapeDtypeStruct((m, d), BF16)],
        compiler_params=_cparams(1),
        name="mix_out",
    )(a, b, sg, sg, x2, wa, wb, wo, g)


FFN_SUB_ROWS = 1024


def _ffn_up_kernel(h_ref, wg_ref, wu_ref, cw_ref, cb_ref, o_ref, wgb_ref, wub_ref, tail_ref,
                   *, blocks_per_seq):
    i = pl.program_id(1)

    @pl.when(i == 0)
    def _():
        wgb_ref[...] = wg_ref[...].astype(BF16)
        wub_ref[...] = wu_ref[...].astype(BF16)

    @pl.when(i % blocks_per_seq == 0)
    def _():
        tail_ref[...] = jnp.zeros_like(tail_ref)

    tail = tail_ref[...]
    for r in range(h_ref.shape[0] // FFN_SUB_ROWS):
        rows = slice(r * FFN_SUB_ROWS, (r + 1) * FFN_SUB_ROWS)
        h = h_ref[rows, :]
        gate = _dot(h, wgb_ref[...])
        up = _dot(h, wub_ref[...])
        ext = jnp.concatenate([tail, gate], axis=0)
        tail = gate[FFN_SUB_ROWS - SUBLANES:, :]
        g_t1 = pltpu.roll(ext, 1, 0)[SUBLANES:, :]
        g_t2 = pltpu.roll(ext, 2, 0)[SUBLANES:, :]
        a = g_t2 * cw_ref[0:1, :]
        a = a + g_t1 * cw_ref[1:2, :]
        a = a + gate * cw_ref[2:3, :]
        a = a + cb_ref[...]
        o_ref[rows, :] = (jax.nn.gelu(a) * up).astype(o_ref.dtype)
    tail_ref[...] = tail


def _ffn_up(h, wg, wu, layer, cw, cb, seq, tm=2048, tf=512):
    m, d = h.shape
    f = wg.shape[2]
    kern = functools.partial(_ffn_up_kernel, blocks_per_seq=seq // tm)
    return pl.pallas_call(
        kern,
        grid=(f // tf, m // tm),
        in_specs=[
            pl.BlockSpec((tm, d), lambda j, i: (i, 0)),
            pl.BlockSpec((None, d, tf), lambda j, i: (layer, 0, j)),
            pl.BlockSpec((None, d, tf), lambda j, i: (layer, 0, j)),
            pl.BlockSpec((CONV_W, tf), lambda j, i: (0, j)),
            pl.BlockSpec((1, tf), lambda j, i: (0, j)),
        ],
        out_specs=pl.BlockSpec((tm, tf), lambda j, i: (i, j)),
        out_shape=jax.ShapeDtypeStruct((m, f), BF16),
        scratch_shapes=[pltpu.VMEM((d, tf), BF16), pltpu.VMEM((d, tf), BF16),
                        pltpu.VMEM((SUBLANES, tf), F32)],
        compiler_params=_cparams(2),
        name="ffn_up",
    )(h, wg, wu, cw, cb)


def _ffn_down_kernel(a_ref, x_ref, w_ref, o_ref):
    o_ref[...] = x_ref[...] + _dot(a_ref[...], w_ref[...])


def _ffn_down(act, x2, w, tm=512):
    m, d = x2.shape
    return pl.pallas_call(
        _ffn_down_kernel,
        grid=(m // tm,),
        in_specs=[
            pl.BlockSpec((tm, act.shape[1]), lambda i: (i, 0)),
            pl.BlockSpec((tm, d), lambda i: (i, 0)),
            _resident(w.shape),
        ],
        out_specs=pl.BlockSpec((tm, d), lambda i: (i, 0)),
        out_shape=jax.ShapeDtypeStruct((m, d), F32),
        compiler_params=_cparams(1),
        name="ffn_down",
    )(act, x2, w)


def _ple_kernel(x_ref, p_ref, g_ref, wg_ref, wp_ref, gn_ref, *out_refs):
    x = x_ref[...]
    h = _rms_rows(x, g_ref[...]).astype(BF16)
    gate = jax.nn.sigmoid(_dot(h, wg_ref[...]))
    proj = _dot(p_ref[...].astype(BF16), wp_ref[...])
    xn = x + proj * gate
    if len(out_refs) == 2:
        out_refs[0][...] = xn
    out_refs[-1][...] = _rms_rows(xn, gn_ref[...]).astype(out_refs[-1].dtype)


def _ple(x2, p2, g, wg, wp, g_next, last, tm=512):
    m, d = x2.shape
    row_spec = pl.BlockSpec((tm, d), lambda i: (i, 0))
    if last:
        out_specs, out_shape = [row_spec], [jax.ShapeDtypeStruct((m, d), F32)]
    else:
        out_specs = [row_spec, row_spec]
        out_shape = [jax.ShapeDtypeStruct((m, d), F32), jax.ShapeDtypeStruct((m, d), BF16)]
    return pl.pallas_call(
        _ple_kernel,
        grid=(m // tm,),
        in_specs=[
            row_spec,
            pl.BlockSpec((tm, PLE_DIM), lambda i: (i, 0)),
            _resident(g.shape), _resident(wg.shape), _resident(wp.shape), _resident(g_next.shape),
        ],
        out_specs=out_specs,
        out_shape=out_shape,
        compiler_params=_cparams(1),
        name="ple",
    )(x2, p2, g, wg, wp, g_next)


def _rotary_tables(seq):
    half = RET_DK // 2
    inv = ROPE_THETA ** (-jnp.arange(half, dtype=F32) / half)
    lo = jnp.arange(CHUNK, dtype=jnp.int32).astype(F32)[:, None] * inv[None, :]
    hi = (jnp.arange(seq // CHUNK, dtype=jnp.int32) * CHUNK).astype(F32)[:, None] * inv[None, :]
    cl, sl = jnp.cos(lo)[None], jnp.sin(lo)[None]
    ch, sh = jnp.cos(hi)[:, None, :], jnp.sin(hi)[:, None, :]
    cos = (ch * cl - sh * sl).reshape(seq, half)
    sin = (sh * cl + ch * sl).reshape(seq, half)
    return jnp.concatenate([cos, cos], axis=-1), jnp.concatenate([-sin, sin], axis=-1)


def _retention_tables():
    log_g = jnp.log1p(-jnp.exp2(-5.0 - jnp.arange(RET_HEADS, dtype=F32)))
    idx = jnp.arange(CHUNK, dtype=F32)
    diff = idx[:, None] - idx[None, :]
    causal = diff >= 0
    decay = jnp.where(causal[None], jnp.exp(log_g[:, None, None] * jnp.where(causal, diff, 0.0)[None]), 0.0)
    xi = jnp.exp(log_g[:, None] * (idx[None, :] + 1.0)).T
    zeta = jnp.exp(log_g[:, None] * (CHUNK - 1.0 - idx[None, :])).T
    chunk_decay = jnp.exp(log_g * CHUNK)
    xi_map = jnp.repeat(xi, RET_DV, axis=1)
    zeta_map = jnp.repeat(zeta, RET_DK, axis=1)
    cd = jnp.broadcast_to(chunk_decay[:, None, None], (RET_HEADS, 1, RET_DV))
    return decay, xi_map, zeta_map, cd


def kernel(x, p, mix_norm_g, w_in, sgu_norm_g, sgu_w, sgu_b, ret_norm_g, w_branch_a, w_branch_b, w_out, ffn_norm_g, ffn_w_gate, ffn_w_up, ffn_conv_w, ffn_conv_b, ffn_w_down, ple_norm_g, ple_w_gate, ple_w_proj, final_norm_g):
    batch, seq, d = x.shape
    depth = w_in.shape[0]
    m = batch * seq
    assert d == D_MODEL and seq % 1024 == 0 and w_in.shape[2] == IN_COLS

    cos_t, sin_t = _rotary_tables(seq)
    decay, xi_map, zeta_map, cd = _retention_tables()
    qk_scale = jnp.concatenate([jnp.ones((1, RET_QK_WIDTH), F32),
                                jnp.full((1, RET_QK_WIDTH), RET_DK ** -0.5, F32)], axis=1)
    rot = (cos_t, sin_t, qk_scale, seq)

    x2 = x.reshape(m, d)
    h = _norm_bf16(x2, mix_norm_g[0][None])
    for i in range(depth):
        z, wa, wb, wo, wpg = _in_proj(h, w_in, i, (w_branch_a, w_branch_b, w_out, ple_w_gate), *rot)
        sgu_bias = jnp.repeat(sgu_b[i].T, SGU_GROUP_DIM, axis=1)
        sg, wd, a, b = _gate_mixer(h, w_in, i, COL_G, 2 * D_MODEL, ffn_w_down, z,
                                   sgu_norm_g[i][None], sgu_w[i], sgu_bias, ret_norm_g[i][None],
                                   decay, xi_map, zeta_map, cd, seq)
        x2, hf = _mix_out(a, b, sg, x2, wa, wb, wo, ffn_norm_g[i][None])
        act = _ffn_up(hf, ffn_w_gate, ffn_w_up, i, ffn_conv_w[i], ffn_conv_b[i][None], seq)
        x2 = _ffn_down(act, x2, wd)
        last = i == depth - 1
        g_next = final_norm_g[None] if last else mix_norm_g[i + 1][None]
        outs = _ple(x2, p[i].reshape(m, PLE_DIM), ple_norm_g[i][None], wpg, ple_w_proj[i].astype(BF16),
                    g_next, last)
        if last:
            (out,) = outs
        else:
            x2, h = outs
    return out.reshape(batch, seq, d)
```

```python
import functools

import jax
import jax.numpy as jnp
from jax import lax
from jax.experimental import pallas as pl
from jax.experimental.pallas import tpu as pltpu

D_MODEL = 2048
CHUNK = 128
SGU_GROUPS = 8
SGU_GROUP_DIM = 128
SGU_WIDTH = SGU_GROUPS * SGU_GROUP_DIM
RET_HEADS = 8
RET_DK = 128
RET_DV = 256
RET_QK_WIDTH = RET_HEADS * RET_DK
RET_V_WIDTH = RET_HEADS * RET_DV
ROPE_THETA = 10000.0
D_FF = 5632
CONV_W = 3
PLE_DIM = 256
NORM_EPS = 1e-6
IN_COLS = 2 * SGU_WIDTH + 2 * RET_QK_WIDTH + 2 * RET_V_WIDTH + 2 * D_MODEL

V7X_VMEM_LIMIT_BYTES = 62 * 1024 * 1024
SUBLANES = 8
BF16_ROWS = 16

BF16 = jnp.bfloat16
F32 = jnp.float32

COL_UV = 0
COL_QK = 2 * SGU_WIDTH
COL_RV = COL_QK + 2 * RET_QK_WIDTH
COL_RG = COL_RV + RET_V_WIDTH
COL_G = COL_RG + RET_V_WIDTH


def _cparams(n_axes, fuse_inputs=None):
    return pltpu.CompilerParams(
        dimension_semantics=("arbitrary",) * n_axes,
        vmem_limit_bytes=V7X_VMEM_LIMIT_BYTES,
        allow_input_fusion=fuse_inputs,
    )


def _rms_rows(x, g):
    y = x * lax.rsqrt(jnp.mean(x * x, axis=-1, keepdims=True) + NORM_EPS)
    return y * g


def _dot(a, b):
    return jnp.dot(a, b, preferred_element_type=F32)


def _norm_kernel(x_ref, g_ref, o_ref):
    o_ref[...] = _rms_rows(x_ref[...], g_ref[...]).astype(o_ref.dtype)


def _norm_bf16(x2, g, tm=1024):
    m, d = x2.shape
    return pl.pallas_call(
        _norm_kernel,
        grid=(m // tm,),
        in_specs=[pl.BlockSpec((tm, d), lambda i: (i, 0)), pl.BlockSpec((1, d), lambda i: (0, 0))],
        out_specs=pl.BlockSpec((tm, d), lambda i: (i, 0)),
        out_shape=jax.ShapeDtypeStruct((m, d), BF16),
        compiler_params=_cparams(1),
        name="norm",
    )(x2, g)


PROJ_SUB_ROWS = 256


def _cast_weight_once(w_ref, wb_ref):
    @pl.when(pl.program_id(1) == 0)
    def _():
        wb_ref[...] = w_ref[...].astype(BF16)


def _cast_sides(side_refs):
    n = len(side_refs) // 2
    for src, dst in zip(side_refs[:n], side_refs[n:]):
        dst[...] = src[...].astype(dst.dtype)


def _side_specs(sides, layer, n_steps, inner):
    in_specs, out_specs, out_shapes = [], [], []
    for w in sides:
        _, rows, cols = w.shape
        slab = rows // n_steps
        assert slab * n_steps == rows and slab % BF16_ROWS == 0
        in_specs.append(pl.BlockSpec((None, slab, cols), lambda j, i: (layer, j * inner + i, 0)))
        out_specs.append(pl.BlockSpec((slab, cols), lambda j, i: (j * inner + i, 0)))
        out_shapes.append(jax.ShapeDtypeStruct((rows, cols), BF16))
    return in_specs, out_specs, out_shapes


PROJ_TN = 1024
IN_PROJ_SLAB_ROWS = 1024
PROJ_KIND_BLOCKS = 2
N_PROJ_SIDES = 4


def _proj_kernel(h_ref, w_ref, cos_ref, sin_ref, scale_ref, *rest):
    n = N_PROJ_SIDES
    side_in, o_ref, side_out, wb_ref = rest[:n], rest[n], rest[n + 1:2 * n + 1], rest[2 * n + 1]
    kind = pl.program_id(0) // PROJ_KIND_BLOCKS
    _cast_weight_once(w_ref, wb_ref)
    _cast_sides(side_in + side_out)

    def slabs(sub, act):
        for r in range(h_ref.shape[0] // sub):
            rows = slice(r * sub, (r + 1) * sub)
            act(rows, _dot(h_ref[rows, :], wb_ref[...]))

    def store(fn):
        def act(rows, z):
            o_ref[rows, :] = fn(z).astype(o_ref.dtype)
        return act

    def rotary(rows, z):
        cos = cos_ref[rows, :]
        sin = sin_ref[rows, :]
        for h in range(z.shape[1] // RET_DK):
            sl = slice(h * RET_DK, (h + 1) * RET_DK)
            zh = z[:, sl]
            rot = zh * cos + pltpu.roll(zh, RET_DK // 2, 1) * sin
            o_ref[rows, sl] = (rot * scale_ref[:, sl]).astype(o_ref.dtype)

    @pl.when(kind == 0)
    def _():
        slabs(IN_PROJ_SLAB_ROWS, store(jax.nn.gelu))

    @pl.when(kind == 1)
    def _():
        slabs(IN_PROJ_SLAB_ROWS, rotary)

    @pl.when(kind == 2)
    def _():
        slabs(IN_PROJ_SLAB_ROWS, store(lambda z: z))

    @pl.when(kind == 3)
    def _():
        slabs(IN_PROJ_SLAB_ROWS, store(jax.nn.silu))


def _in_proj(h, w, layer, sides, cos_t, sin_t, scale, seq, tm=2048):
    m, d = h.shape
    tn = PROJ_TN
    grid = (4 * PROJ_KIND_BLOCKS, m // tm)
    n = grid[0] * tn
    assert len(sides) == N_PROJ_SIDES
    s_in, s_out, s_shapes = _side_specs(sides, layer, grid[0] * grid[1], grid[1])
    pos_blocks = seq // tm
    rot_block = lambda j: jnp.clip(j - PROJ_KIND_BLOCKS, 0, PROJ_KIND_BLOCKS - 1)
    return pl.pallas_call(
        _proj_kernel,
        grid=grid,
        in_specs=[
            pl.BlockSpec((tm, d), lambda j, i: (i, 0)),
            pl.BlockSpec((None, d, tn), lambda j, i: (layer, 0, j)),
            pl.BlockSpec((tm, RET_DK), lambda j, i: (i % pos_blocks, 0)),
            pl.BlockSpec((tm, RET_DK), lambda j, i: (i % pos_blocks, 0)),
            pl.BlockSpec((1, tn), lambda j, i: (0, rot_block(j))),
        ] + s_in,
        out_specs=[pl.BlockSpec((tm, tn), lambda j, i: (i, j))] + s_out,
        out_shape=[jax.ShapeDtypeStruct((m, n), BF16)] + s_shapes,
        scratch_shapes=[pltpu.VMEM((d, tn), BF16)],
        compiler_params=_cparams(2),
        name="in_proj",
    )(h, w, cos_t, sin_t, scale, *sides)


MIXER_CHUNKS_PER_STEP = 2


def _sgu_pair(gu_ref, gv_ref, sgu_g_ref, sgu_w_ref, sgu_bias_ref, a_ref):
    row = lax.broadcasted_iota(jnp.int32, (CHUNK, CHUNK), 0)
    col = lax.broadcasted_iota(jnp.int32, (CHUNK, CHUNK), 1)
    causal = row >= col
    chunks = [slice(c * CHUNK, (c + 1) * CHUNK) for c in range(MIXER_CHUNKS_PER_STEP)]
    vn = [_rms_rows(gv_ref[rows, :].astype(F32), sgu_g_ref[...]).astype(BF16) for rows in chunks]
    for g in range(SGU_GROUPS):
        sl = slice(g * SGU_GROUP_DIM, (g + 1) * SGU_GROUP_DIM)
        w = jnp.where(causal, sgu_w_ref[g], 0.0).astype(BF16)
        s = _dot(w, jnp.concatenate([v[:, sl] for v in vn], axis=1))
        for c, rows in enumerate(chunks):
            s_c = s[:, c * SGU_GROUP_DIM:(c + 1) * SGU_GROUP_DIM] + sgu_bias_ref[:, sl]
            a_ref[rows, sl] = (gu_ref[rows, sl].astype(F32) * s_c).astype(a_ref.dtype)


def _retention_scores(c, q_ref, k_ref, rv_ref, xi_ref, zeta_ref, state_ref):
    rows = slice(c * CHUNK, (c + 1) * CHUNK)
    out = []
    for h in range(RET_HEADS):
        ksl = slice(h * RET_DK, (h + 1) * RET_DK)
        vsl = slice(h * RET_DV, (h + 1) * RET_DV)
        q = q_ref[rows, ksl]
        k = k_ref[rows, ksl]
        v = rv_ref[rows, vsl]
        scores = lax.dot_general(q, k, (((1,), (1,)), ((), ())), preferred_element_type=F32)
        kz = (k.astype(F32) * zeta_ref[:, ksl]).astype(BF16)
        prev = state_ref[h]
        q_xi = (q.astype(F32) * xi_ref[:, h * RET_DV:h * RET_DV + RET_DK]).astype(BF16)
        out.append((scores, kz, prev, q_xi, v))
    return out


def _retention_outputs(c, stage1, srg_ref, ret_g_ref, decay_ref, cd_ref, b_ref, state_ref):
    rows = slice(c * CHUNK, (c + 1) * CHUNK)
    for h, (scores, kz, prev, q_xi, v) in enumerate(stage1):
        vsl = slice(h * RET_DV, (h + 1) * RET_DV)
        lhs = jnp.concatenate([(scores * decay_ref[h]).astype(BF16), q_xi], axis=1)
        rhs = jnp.concatenate([v, prev.astype(BF16)], axis=0)
        y = _dot(lhs, rhs)
        kv = lax.dot_general(kz, v, (((0,), (0,)), ((), ())), preferred_element_type=F32)
        state_ref[h] = prev * cd_ref[h] + kv
        y = _rms_rows(y, ret_g_ref[:, vsl])
        b_ref[rows, vsl] = (srg_ref[rows, vsl].astype(F32) * y).astype(b_ref.dtype)


def _gate_mixer_kernel(h_ref, w_ref, wd_ref, gu_ref, gv_ref, q_ref, k_ref, rv_ref, srg_ref,
                       sgu_g_ref, sgu_w_ref, sgu_bias_ref, ret_g_ref, decay_ref, xi_ref, zeta_ref, cd_ref,
                       o_ref, wdo_ref, a_ref, b_ref, wb_ref, state_ref, *, steps_per_seq, sub):
    step = pl.program_id(0) * pl.num_programs(1) + pl.program_id(1)
    _cast_weight_once(w_ref, wb_ref)

    @pl.when(step % steps_per_seq == 0)
    def _():
        state_ref[...] = jnp.zeros_like(state_ref)

    wdo_ref[...] = wd_ref[...].astype(wdo_ref.dtype)

    def gate_slab(r):
        rows = slice(r * sub, (r + 1) * sub)
        o_ref[rows, :] = jax.nn.sigmoid(_dot(h_ref[rows, :], wb_ref[...])).astype(o_ref.dtype)

    assert h_ref.shape[0] // sub == 4 and MIXER_CHUNKS_PER_STEP == 2
    stage1 = _retention_scores(0, q_ref, k_ref, rv_ref, xi_ref, zeta_ref, state_ref)
    gate_slab(0)
    _sgu_pair(gu_ref, gv_ref, sgu_g_ref, sgu_w_ref, sgu_bias_ref, a_ref)
    _retention_outputs(0, stage1, srg_ref, ret_g_ref, decay_ref, cd_ref, b_ref, state_ref)
    gate_slab(1)
    stage1 = _retention_scores(1, q_ref, k_ref, rv_ref, xi_ref, zeta_ref, state_ref)
    gate_slab(2)
    _retention_outputs(1, stage1, srg_ref, ret_g_ref, decay_ref, cd_ref, b_ref, state_ref)
    gate_slab(3)


def _gate_mixer(h, w, layer, col0, n, w_down, z, sgu_g, sgu_w, sgu_bias, ret_g,
                decay, xi_map, zeta_map, cd, seq, tm=1024, tn=1024, sub=PROJ_SUB_ROWS):
    m, d = h.shape
    off = col0 // tn
    grid = (n // tn, m // tm)
    inner = grid[1]
    mix_rows = MIXER_CHUNKS_PER_STEP * CHUNK
    assert grid[0] * grid[1] * mix_rows == m and seq % mix_rows == 0
    (wd_in,), (wd_out,), (wd_shape,) = _side_specs((w_down,), layer, grid[0] * grid[1], inner)

    def rows(width, blk):
        return pl.BlockSpec((mix_rows, width), lambda j, i: (j * inner + i, blk))

    def const(a):
        return pl.BlockSpec(a.shape, lambda j, i: (0,) * a.ndim, pipeline_mode=pl.Buffered(1))

    consts = (sgu_g, sgu_w, sgu_bias, ret_g, decay, xi_map, zeta_map, cd)
    kern = functools.partial(_gate_mixer_kernel, steps_per_seq=seq // mix_rows, sub=sub)
    return pl.pallas_call(
        kern,
        grid=grid,
        in_specs=[
            pl.BlockSpec((tm, d), lambda j, i: (i, 0)),
            pl.BlockSpec((None, d, tn), lambda j, i: (layer, 0, off + j)),
            wd_in,
            rows(SGU_WIDTH, COL_UV // SGU_WIDTH), rows(SGU_WIDTH, COL_UV // SGU_WIDTH + 1),
            rows(RET_QK_WIDTH, COL_QK // RET_QK_WIDTH), rows(RET_QK_WIDTH, COL_QK // RET_QK_WIDTH + 1),
            rows(RET_V_WIDTH, COL_RV // RET_V_WIDTH), rows(RET_V_WIDTH, COL_RG // RET_V_WIDTH),
        ] + [const(c) for c in consts],
        out_specs=[pl.BlockSpec((tm, tn), lambda j, i: (i, j)), wd_out,
                   rows(SGU_WIDTH, 0), rows(RET_V_WIDTH, 0)],
        out_shape=[jax.ShapeDtypeStruct((m, n), BF16), wd_shape,
                   jax.ShapeDtypeStruct((m, SGU_WIDTH), BF16),
                   jax.ShapeDtypeStruct((m, RET_V_WIDTH), BF16)],
        scratch_shapes=[pltpu.VMEM((d, tn), BF16), pltpu.VMEM((RET_HEADS, RET_DK, RET_DV), F32)],
        compiler_params=_cparams(2),
        name="gate_mixer",
    )(h, w, w_down, z, z, z, z, z, z, *consts)


def _resident(shape):
    return pl.BlockSpec(shape, lambda *_: (0,) * len(shape), pipeline_mode=pl.Buffered(1))


def _mix_out_kernel(a_ref, b_ref, sga_ref, sgb_ref, x_ref, wa_ref, wb_ref, wo_ref, g_ref, xo_ref, h_ref):
    ya = _dot(a_ref[...], wa_ref[...])
    yb = _dot(b_ref[...], wb_ref[...])
    m = (sga_ref[...].astype(F32) * ya + sgb_ref[...].astype(F32) * yb).astype(BF16)
    xn = x_ref[...] + _dot(m, wo_ref[...])
    xo_ref[...] = xn
    h_ref[...] = _rms_rows(xn, g_ref[...]).astype(h_ref.dtype)


def _mix_out(a, b, sg, x2, wa, wb, wo, g, tm=512):
    m, d = x2.shape
    return pl.pallas_call(
        _mix_out_kernel,
        grid=(m // tm,),
        in_specs=[
            pl.BlockSpec((tm, a.shape[1]), lambda i: (i, 0)),
            pl.BlockSpec((tm, b.shape[1]), lambda i: (i, 0)),
            pl.BlockSpec((tm, d), lambda i: (i, 0)),
            pl.BlockSpec((tm, d), lambda i: (i, 1)),
            pl.BlockSpec((tm, d), lambda i: (i, 0)),
            _resident(wa.shape), _resident(wb.shape), _resident(wo.shape), _resident(g.shape),
        ],
        out_specs=[pl.BlockSpec((tm, d), lambda i: (i, 0)), pl.BlockSpec((tm, d), lambda i: (i, 0))],
        out_shape=[jax.ShapeDtypeStruct((m, d), F32), jax.ShapeDtypeStruct((m, d), BF16)],
        compiler_params=_cparams(1),
        name="mix_out",
    )(a, b, sg, sg, x2, wa, wb, wo, g)


FFN_SUB_ROWS = 1024


def _ffn_up_kernel(h_ref, wg_ref, wu_ref, cw_ref, cb_ref, o_ref, wgb_ref, wub_ref, tail_ref,
                   *, blocks_per_seq):
    i = pl.program_id(1)

    @pl.when(i == 0)
    def _():
        wgb_ref[...] = wg_ref[...].astype(BF16)
        wub_ref[...] = wu_ref[...].astype(BF16)

    @pl.when(i % blocks_per_seq == 0)
    def _():
        tail_ref[...] = jnp.zeros_like(tail_ref)

    tail = tail_ref[...]
    for r in range(h_ref.shape[0] // FFN_SUB_ROWS):
        rows = slice(r * FFN_SUB_ROWS, (r + 1) * FFN_SUB_ROWS)
        h = h_ref[rows, :]
        gate = _dot(h, wgb_ref[...])
        up = _dot(h, wub_ref[...])
        ext = jnp.concatenate([tail, gate], axis=0)
        tail = gate[FFN_SUB_ROWS - SUBLANES:, :]
        g_t1 = pltpu.roll(ext, 1, 0)[SUBLANES:, :]
        g_t2 = pltpu.roll(ext, 2, 0)[SUBLANES:, :]
        a = g_t2 * cw_ref[0:1, :]
        a = a + g_t1 * cw_ref[1:2, :]
        a = a + gate * cw_ref[2:3, :]
        a = a + cb_ref[...]
        o_ref[rows, :] = (jax.nn.gelu(a) * up).astype(o_ref.dtype)
    tail_ref[...] = tail


def _ffn_up(h, wg, wu, layer, cw, cb, seq, tm=2048, tf=512):
    m, d = h.shape
    f = wg.shape[2]
    kern = functools.partial(_ffn_up_kernel, blocks_per_seq=seq // tm)
    return pl.pallas_call(
        kern,
        grid=(f // tf, m // tm),
        in_specs=[
            pl.BlockSpec((tm, d), lambda j, i: (i, 0)),
            pl.BlockSpec((None, d, tf), lambda j, i: (layer, 0, j)),
            pl.BlockSpec((None, d, tf), lambda j, i: (layer, 0, j)),
            pl.BlockSpec((CONV_W, tf), lambda j, i: (0, j)),
            pl.BlockSpec((1, tf), lambda j, i: (0, j)),
        ],
        out_specs=pl.BlockSpec((tm, tf), lambda j, i: (i, j)),
        out_shape=jax.ShapeDtypeStruct((m, f), BF16),
        scratch_shapes=[pltpu.VMEM((d, tf), BF16), pltpu.VMEM((d, tf), BF16),
                        pltpu.VMEM((SUBLANES, tf), F32)],
        compiler_params=_cparams(2),
        name="ffn_up",
    )(h, wg, wu, cw, cb)


def _ffn_down_kernel(a_ref, x_ref, w_ref, o_ref):
    o_ref[...] = x_ref[...] + _dot(a_ref[...], w_ref[...])


def _ffn_down(act, x2, w, tm=512):
    m, d = x2.shape
    return pl.pallas_call(
        _ffn_down_kernel,
        grid=(m // tm,),
        in_specs=[
            pl.BlockSpec((tm, act.shape[1]), lambda i: (i, 0)),
            pl.BlockSpec((tm, d), lambda i: (i, 0)),
            _resident(w.shape),
        ],
        out_specs=pl.BlockSpec((tm, d), lambda i: (i, 0)),
        out_shape=jax.ShapeDtypeStruct((m, d), F32),
        compiler_params=_cparams(1),
        name="ffn_down",
    )(act, x2, w)


def _ple_kernel(x_ref, p_ref, g_ref, wg_ref, wp_ref, gn_ref, *out_refs):
    x = x_ref[...]
    h = _rms_rows(x, g_ref[...]).astype(BF16)
    gate = jax.nn.sigmoid(_dot(h, wg_ref[...]))
    proj = _dot(p_ref[...].astype(BF16), wp_ref[...])
    xn = x + proj * gate
    if len(out_refs) == 2:
        out_refs[0][...] = xn
    out_refs[-1][...] = _rms_rows(xn, gn_ref[...]).astype(out_refs[-1].dtype)


def _ple(x2, p2, g, wg, wp, g_next, last, tm=512):
    m, d = x2.shape
    row_spec = pl.BlockSpec((tm, d), lambda i: (i, 0))
    if last:
        out_specs, out_shape = [row_spec], [jax.ShapeDtypeStruct((m, d), F32)]
    else:
        out_specs = [row_spec, row_spec]
        out_shape = [jax.ShapeDtypeStruct((m, d), F32), jax.ShapeDtypeStruct((m, d), BF16)]
    return pl.pallas_call(
        _ple_kernel,
        grid=(m // tm,),
        in_specs=[
            row_spec,
            pl.BlockSpec((tm, PLE_DIM), lambda i: (i, 0)),
            _resident(g.shape), _resident(wg.shape), _resident(wp.shape), _resident(g_next.shape),
        ],
        out_specs=out_specs,
        out_shape=out_shape,
        compiler_params=_cparams(1, [False, False, True, False, True, True]),
        name="ple",
    )(x2, p2, g, wg, wp, g_next)


def _rotary_tables(seq):
    half = RET_DK // 2
    inv = ROPE_THETA ** (-jnp.arange(half, dtype=F32) / half)
    lo = jnp.arange(CHUNK, dtype=jnp.int32).astype(F32)[:, None] * inv[None, :]
    hi = (jnp.arange(seq // CHUNK, dtype=jnp.int32) * CHUNK).astype(F32)[:, None] * inv[None, :]
    cl, sl = jnp.cos(lo)[None], jnp.sin(lo)[None]
    ch, sh = jnp.cos(hi)[:, None, :], jnp.sin(hi)[:, None, :]
    cos = (ch * cl - sh * sl).reshape(seq, half)
    sin = (sh * cl + ch * sl).reshape(seq, half)
    return jnp.concatenate([cos, cos], axis=-1), jnp.concatenate([-sin, sin], axis=-1)


def _retention_tables():
    log_g = jnp.log1p(-jnp.exp2(-5.0 - jnp.arange(RET_HEADS, dtype=F32)))
    idx = jnp.arange(CHUNK, dtype=F32)
    diff = idx[:, None] - idx[None, :]
    causal = diff >= 0
    decay = jnp.where(causal[None], jnp.exp(log_g[:, None, None] * jnp.where(causal, diff, 0.0)[None]), 0.0)
    xi = jnp.exp(log_g[:, None] * (idx[None, :] + 1.0)).T
    zeta = jnp.exp(log_g[:, None] * (CHUNK - 1.0 - idx[None, :])).T
    chunk_decay = jnp.exp(log_g * CHUNK)
    xi_map = jnp.repeat(xi, RET_DV, axis=1)
    zeta_map = jnp.repeat(zeta, RET_DK, axis=1)
    cd = jnp.broadcast_to(chunk_decay[:, None, None], (RET_HEADS, 1, RET_DV))
    return decay, xi_map, zeta_map, cd


def kernel(x, p, mix_norm_g, w_in, sgu_norm_g, sgu_w, sgu_b, ret_norm_g, w_branch_a, w_branch_b, w_out, ffn_norm_g, ffn_w_gate, ffn_w_up, ffn_conv_w, ffn_conv_b, ffn_w_down, ple_norm_g, ple_w_gate, ple_w_proj, final_norm_g):
    batch, seq, d = x.shape
    depth = w_in.shape[0]
    m = batch * seq
    assert d == D_MODEL and seq % 1024 == 0 and w_in.shape[2] == IN_COLS

    cos_t, sin_t = _rotary_tables(seq)
    decay, xi_map, zeta_map, cd = _retention_tables()
    qk_scale = jnp.concatenate([jnp.ones((1, RET_QK_WIDTH), F32),
                                jnp.full((1, RET_QK_WIDTH), RET_DK ** -0.5, F32)], axis=1)
    rot = (cos_t, sin_t, qk_scale, seq)

    x2 = x.reshape(m, d)
    h = _norm_bf16(x2, mix_norm_g[0][None])
    for i in range(depth):
        z, wa, wb, wo, wpg = _in_proj(h, w_in, i, (w_branch_a, w_branch_b, w_out, ple_w_gate), *rot)
        sgu_bias = jnp.repeat(sgu_b[i].T, SGU_GROUP_DIM, axis=1)
        sg, wd, a, b = _gate_mixer(h, w_in, i, COL_G, 2 * D_MODEL, ffn_w_down, z,
                                   sgu_norm_g[i][None], sgu_w[i], sgu_bias, ret_norm_g[i][None],
                                   decay, xi_map, zeta_map, cd, seq)
        x2, hf = _mix_out(a, b, sg, x2, wa, wb, wo, ffn_norm_g[i][None])
        act = _ffn_up(hf, ffn_w_gate, ffn_w_up, i, ffn_conv_w[i], ffn_conv_b[i][None], seq)
        x2 = _ffn_down(act, x2, wd)
        last = i == depth - 1
        g_next = final_norm_g[None] if last else mix_norm_g[i + 1][None]
        outs = _ple(x2, p[i].reshape(m, PLE_DIM), ple_norm_g[i][None], wpg, ple_w_proj[i].astype(BF16),
                    g_next, last)
        if last:
            (out,) = outs
        else:
            x2, h = outs
    return out.reshape(batch, seq, d)
```
